```python
import math
import jax, jax.numpy as jnp
from jax import lax
import numpy as np

D_MODEL = 2048
BATCH = 2
SEQ = 4096
DEPTH = 4

LRU_WIDTH = D_MODEL // 4
LRU_BLOCKS = 8
LRU_BLOCK_DIM = LRU_WIDTH // LRU_BLOCKS
LRU_CONV = 4
LRU_C = 8.0
RWKV_HEAD_DIM = 64
RWKV_HEADS = (3 * D_MODEL) // (8 * RWKV_HEAD_DIM)
RWKV_WIDTH = RWKV_HEADS * RWKV_HEAD_DIM
RWKV_DECAY_LORA = 64
RWKV_ICLR_LORA = 64
RWKV_DECAY_SCALE = math.exp(-0.5)
RWKV_GN_EPS = 64e-5
MOBA_HEAD_DIM = 64
MOBA_HEADS = (3 * D_MODEL) // (8 * MOBA_HEAD_DIM)
MOBA_WIDTH = MOBA_HEADS * MOBA_HEAD_DIM
MOBA_BLOCK = 256
MOBA_TOPK = 3
MOBA_QCHUNK = 64
MIX_WIDTH = LRU_WIDTH + RWKV_WIDTH + MOBA_WIDTH
REL_BUCKETS = 32
REL_MAX_DIST = 128
NORM_EPS = 1e-6

RWKV_STREAM = 3 * RWKV_WIDTH + RWKV_DECAY_LORA + RWKV_ICLR_LORA
IN_SPLITS = (LRU_WIDTH, LRU_WIDTH, RWKV_STREAM, RWKV_WIDTH, 3 * MOBA_WIDTH, MOBA_WIDTH)
IN_COLS = sum(IN_SPLITS)
IN_OFFSETS = tuple(int(o) for o in np.cumsum(IN_SPLITS)[:-1])
RWKV_OFFSETS = (RWKV_WIDTH, 2 * RWKV_WIDTH, 3 * RWKV_WIDTH, 3 * RWKV_WIDTH + RWKV_DECAY_LORA)

kernel_name = 'hybrid_lru_rwkv7_moba_block'


def rms_norm(x, w):
    xf = x.astype(jnp.float32)
    y = xf * lax.rsqrt(jnp.mean(xf * xf, axis=-1, keepdims=True) + NORM_EPS)
    return (y * w.astype(jnp.float32)).astype(x.dtype)


def rg_lru_branch(xa, conv_w, conv_b, ga_w, ga_b, gx_w, gx_b, lam):
    B, T, C = xa.shape
    xf = xa.astype(jnp.float32)
    xc = lax.conv_general_dilated(
        xf, conv_w.astype(jnp.float32)[:, None, :], window_strides=(1,),
        padding=[(LRU_CONV - 1, 0)], dimension_numbers=('NWC', 'WIO', 'NWC'),
        feature_group_count=C) + conv_b.astype(jnp.float32)
    xg = xc.reshape(B, T, LRU_BLOCKS, LRU_BLOCK_DIM)
    r = jax.nn.sigmoid(jnp.einsum('btgi,gij->btgj', xg, ga_w.astype(jnp.float32)).reshape(B, T, C) + ga_b)
    i = jax.nn.sigmoid(jnp.einsum('btgi,gij->btgj', xg, gx_w.astype(jnp.float32)).reshape(B, T, C) + gx_b)
    log_a = -LRU_C * r * jax.nn.softplus(-lam.astype(jnp.float32))
    a = jnp.exp(log_a)
    b = jnp.sqrt(-jnp.expm1(2.0 * log_a)) * (i * xc)

    def combine(left, right):
        a1, b1 = left
        a2, b2 = right
        return a1 * a2, a2 * b1 + b2

    _, h = lax.associative_scan(combine, (a, b), axis=1)
    return h


def token_shift(p):
    return jnp.pad(p, ((0, 0), (1, 0), (0, 0)))[:, :-1]


def rwkv7_branch(stream, mix, w0, w_up, a0, a_up, k_k, k_a, r_k, ln_w, ln_b):
    B, T, _ = stream.shape
    H, N = RWKV_HEADS, RWKV_HEAD_DIM
    s = stream.astype(jnp.float32)
    s = s + mix.astype(jnp.float32) * (token_shift(s) - s)
    r, k, v, wd, ad = jnp.split(s, RWKV_OFFSETS, axis=-1)
    log_w = -RWKV_DECAY_SCALE * jax.nn.sigmoid(w0 + jnp.tanh(wd) @ w_up.astype(jnp.float32))
    a = jax.nn.sigmoid(a0 + ad @ a_up.astype(jnp.float32))
    heads = lambda t: t.reshape(B, T, H, N)
    kk = heads(k * k_k)
    kk = kk / jnp.maximum(jnp.sqrt(jnp.sum(kk * kk, axis=-1, keepdims=True)), 1e-12)
    k = k * (1.0 + (a - 1.0) * k_a)
    r, k, v, a, w = heads(r), heads(k), heads(v), heads(a), heads(jnp.exp(log_w))

    def step(S, inp):
        r_t, w_t, k_t, v_t, kk_t, a_t = inp
        S = (S * w_t[:, :, None, :]
             + jnp.einsum('bhij,bhj->bhi', S, -kk_t)[..., None] * (kk_t * a_t)[:, :, None, :]
             + v_t[..., None] * k_t[:, :, None, :])
        return S, jnp.einsum('bhij,bhj->bhi', S, r_t)

    xs = tuple(jnp.moveaxis(t, 1, 0) for t in (r, w, k, v, kk, a))
    _, y = lax.scan(step, jnp.zeros((B, H, N, N), jnp.float32), xs)
    y = jnp.moveaxis(y, 0, 1)
    mu = jnp.mean(y, axis=-1, keepdims=True)
    var = jnp.mean(jnp.square(y - mu), axis=-1, keepdims=True)
    yn = ((y - mu) * lax.rsqrt(var + RWKV_GN_EPS)).reshape(B, T, RWKV_WIDTH) * ln_w + ln_b
    bonus = jnp.sum(r * k * r_k.astype(jnp.float32), axis=-1, keepdims=True) * v
    return yn + bonus.reshape(B, T, RWKV_WIDTH)


def rel_bucket(dist):
    n = jnp.maximum(dist, 0)
    max_exact = REL_BUCKETS // 2
    large = max_exact + (jnp.log(jnp.maximum(n, 1).astype(jnp.float32) / max_exact)
                         / math.log(REL_MAX_DIST / max_exact) * (REL_BUCKETS - max_exact)).astype(jnp.int32)
    large = jnp.minimum(large, REL_BUCKETS - 1)
    return jnp.where(n < max_exact, n, large)


def moba_branch(qkv, rel_bias):
    B, T, _ = qkv.shape
    H, Dh, BS, QC = MOBA_HEADS, MOBA_HEAD_DIM, MOBA_BLOCK, MOBA_QCHUNK
    t_pad = -(-T // BS) * BS

    def heads(t):
        t = t.astype(jnp.float32).reshape(B, T, H, Dh).transpose(0, 2, 1, 3)
        return jnp.pad(t, ((0, 0), (0, 0), (0, t_pad - T), (0, 0)))

    q, k, v = (heads(t) for t in jnp.split(qkv, 3, axis=-1))
    nb = t_pad // BS
    kb = k.reshape(B, H, nb, BS, Dh)
    vb = v.reshape(B, H, nb, BS, Dh)
    kmean = jnp.mean(kb, axis=3)
    gate = jnp.einsum('bhtd,bhnd->bhtn', q, kmean)
    qblk = jnp.arange(t_pad) // BS
    gate = jnp.where(jnp.arange(nb)[None, :] < qblk[:, None], gate, -jnp.inf)
    k_sel = min(MOBA_TOPK, nb)
    _, sel = lax.top_k(gate, k_sel)
    scale = Dh ** -0.5
    bi = jnp.arange(B)[:, None, None, None]
    hi = jnp.arange(H)[None, :, None, None]
    hi5 = jnp.arange(H)[None, :, None, None, None]
    rb = rel_bias.astype(jnp.float32)

    def chunk(c):
        start = c * QC
        qc = lax.dynamic_slice_in_dim(q, start, QC, axis=2)
        sc = lax.dynamic_slice_in_dim(sel, start, QC, axis=2)
        qpos = start + jnp.arange(QC)
        own = start // BS
        k_g = kb[bi, hi, sc]
        v_g = vb[bi, hi, sc]
        k_o = lax.dynamic_index_in_dim(kb, own, axis=2, keepdims=False)
        v_o = lax.dynamic_index_in_dim(vb, own, axis=2, keepdims=False)
        kpos_g = sc[..., None] * BS + jnp.arange(BS)
        kpos_o = own * BS + jnp.arange(BS)
        bias_g = rb[rel_bucket(qpos[:, None, None] - kpos_g), hi5]
        bias_o = jnp.transpose(rb[rel_bucket(qpos[:, None] - kpos_o[None, :])], (2, 0, 1))
        s_g = jnp.einsum('bhqd,bhqkjd->bhqkj', qc, k_g) * scale + bias_g
        s_o = jnp.einsum('bhqd,bhjd->bhqj', qc, k_o) * scale + bias_o
        ok_g = jnp.arange(k_sel)[None, :] < (qpos // BS)[:, None]
        ok_o = kpos_o[None, :] <= qpos[:, None]
        logits = jnp.concatenate(
            [jnp.where(ok_g[:, :, None], s_g, -jnp.inf).reshape(B, H, QC, k_sel * BS),
             jnp.where(ok_o, s_o, -jnp.inf)], axis=-1)
        p = jax.nn.softmax(logits, axis=-1)
        p_g = p[..., :k_sel * BS].reshape(B, H, QC, k_sel, BS)
        p_o = p[..., k_sel * BS:]
        return (jnp.einsum('bhqkj,bhqkjd->bhqd', p_g, v_g)
                + jnp.einsum('bhqj,bhjd->bhqd', p_o, v_o))

    out = lax.map(chunk, jnp.arange(t_pad // QC))
    out = out.transpose(1, 0, 3, 2, 4).reshape(B, t_pad, MOBA_WIDTH)
    return out[:, :T]


def setup_inputs(seed: int = 0) -> dict:
    key = jax.random.key(seed)
    ks = jax.random.split(key, 23)
    nrm = lambda k, shape: jax.random.normal(k, shape, jnp.float32)
    u = jax.random.uniform(ks[10], (DEPTH, LRU_WIDTH), jnp.float32, minval=0.9, maxval=0.999)
    a_base = u ** (1.0 / LRU_C)
    return {
        'x': nrm(ks[0], (BATCH, SEQ, D_MODEL)),
        'norm_w': 1.0 + 0.02 * nrm(ks[1], (DEPTH, D_MODEL)),
        'w_in': nrm(ks[2], (DEPTH, D_MODEL, IN_COLS)) * D_MODEL ** -0.5,
        'w_out': nrm(ks[3], (DEPTH, MIX_WIDTH, D_MODEL)) * MIX_WIDTH ** -0.5,
        'lru_conv_w': nrm(ks[4], (DEPTH, LRU_CONV, LRU_WIDTH)) * LRU_CONV ** -0.5,
        'lru_conv_b': 0.02 * nrm(ks[5], (DEPTH, LRU_WIDTH)),
        'lru_gate_a_w': nrm(ks[6], (DEPTH, LRU_BLOCKS, LRU_BLOCK_DIM, LRU_BLOCK_DIM)) * LRU_BLOCK_DIM ** -0.5,
        'lru_gate_a_b': 0.02 * nrm(ks[7], (DEPTH, LRU_WIDTH)),
        'lru_gate_x_w': nrm(ks[8], (DEPTH, LRU_BLOCKS, LRU_BLOCK_DIM, LRU_BLOCK_DIM)) * LRU_BLOCK_DIM ** -0.5,
        'lru_gate_x_b': 0.02 * nrm(ks[9], (DEPTH, LRU_WIDTH)),
        'lru_lambda': jnp.log(a_base) - jnp.log1p(-a_base),
        'rwkv_mix': jax.random.uniform(ks[11], (DEPTH, RWKV_STREAM), jnp.float32),
        'rwkv_w0': jax.random.uniform(ks[12], (DEPTH, RWKV_WIDTH), jnp.float32, minval=-6.0, maxval=1.0),
        'rwkv_w_up': 0.5 * nrm(ks[13], (DEPTH, RWKV_DECAY_LORA, RWKV_WIDTH)) * RWKV_DECAY_LORA ** -0.5,
        'rwkv_a0': 0.1 * nrm(ks[14], (DEPTH, RWKV_WIDTH)),
        'rwkv_a_up': 0.5 * nrm(ks[15], (DEPTH, RWKV_ICLR_LORA, RWKV_WIDTH)) * RWKV_ICLR_LORA ** -0.5,
        'rwkv_k_k': 0.85 + 0.05 * nrm(ks[16], (DEPTH, RWKV_WIDTH)),
        'rwkv_k_a': 1.0 + 0.05 * nrm(ks[17], (DEPTH, RWKV_WIDTH)),
        'rwkv_r_k': 0.1 * nrm(ks[18], (DEPTH, RWKV_HEADS, RWKV_HEAD_DIM)),
        'rwkv_ln_w': 1.0 + 0.02 * nrm(ks[19], (DEPTH, RWKV_WIDTH)),
        'rwkv_ln_b': 0.02 * nrm(ks[20], (DEPTH, RWKV_WIDTH)),
        'rel_bias': 0.1 * nrm(ks[21], (REL_BUCKETS, MOBA_HEADS)),
        'final_norm_w': 1.0 + 0.02 * nrm(ks[22], (D_MODEL,)),
    }


def reference(x, norm_w, w_in, w_out, lru_conv_w, lru_conv_b, lru_gate_a_w, lru_gate_a_b,
              lru_gate_x_w, lru_gate_x_b, lru_lambda, rwkv_mix, rwkv_w0, rwkv_w_up, rwkv_a0,
              rwkv_a_up, rwkv_k_k, rwkv_k_a, rwkv_r_k, rwkv_ln_w, rwkv_ln_b, rel_bias, final_norm_w):
    for l in range(DEPTH):
        h = rms_norm(x, norm_w[l])
        p = h @ w_in[l]
        xa, ga, sb, gb, qkv, gc = jnp.split(p, IN_OFFSETS, axis=-1)
        y_a = rg_lru_branch(xa, lru_conv_w[l], lru_conv_b[l], lru_gate_a_w[l], lru_gate_a_b[l],
                            lru_gate_x_w[l], lru_gate_x_b[l], lru_lambda[l]) * jax.nn.silu(ga.astype(jnp.float32))
        y_b = rwkv7_branch(sb, rwkv_mix[l], rwkv_w0[l], rwkv_w_up[l], rwkv_a0[l], rwkv_a_up[l],
                           rwkv_k_k[l], rwkv_k_a[l], rwkv_r_k[l], rwkv_ln_w[l],
                           rwkv_ln_b[l]) * jax.nn.silu(gb.astype(jnp.float32))
        y_c = moba_branch(qkv, rel_bias) * jax.nn.silu(gc.astype(jnp.float32))
        y = jnp.concatenate([y_a, y_b, y_c], axis=-1).astype(x.dtype)
        x = x + y @ w_out[l]
    return rms_norm(x, final_norm_w)
```

```python
import functools
import math

import numpy as np
import jax
import jax.numpy as jnp
from jax import lax
from jax.experimental import pallas as pl
from jax.experimental.pallas import tpu as pltpu

F32 = jnp.float32
BF16 = jnp.bfloat16

LANES = 128
HEAD_DIM = 64
NORM_EPS = 1e-6
LRU_CONV = 4
LRU_C = 8.0
RWKV_LORA = 64
RWKV_DECAY_SCALE = math.exp(-0.5)
RWKV_GN_EPS = 64e-5
RWKV_CHUNK = 64
MOBA_BLOCK = 256
MOBA_TOPK = 3
REL_BUCKETS = 32
REL_MAX_DIST = 128
VMEM_LIMIT = 56 * 1024 * 1024


def _dot(a, b, ca=1, cb=0):
    return lax.dot_general(a.astype(BF16), b.astype(BF16), (((ca,), (cb,)), ((), ())),
                           preferred_element_type=F32)


def _sigmoid(x):
    return 1.0 / (1.0 + jnp.exp(-x))


def _silu(x):
    return x * _sigmoid(x)


def _inproj_body(x_ref, nw_ref, w_ref, o_ref, h_ref):
    @pl.when(pl.program_id(1) == 0)
    def _():
        x = x_ref[...]
        ms = jnp.mean(x * x, axis=-1, keepdims=True)
        h_ref[...] = (x * lax.rsqrt(ms + NORM_EPS) * nw_ref[...]).astype(BF16)

    o_ref[...] = jnp.dot(h_ref[...], w_ref[...], preferred_element_type=F32)


def _in_proj(x2, nw, w, tm, tn):
    M, D = x2.shape
    N = w.shape[1]
    return pl.pallas_call(
        _inproj_body,
        grid=(M // tm, N // tn),
        in_specs=[pl.BlockSpec((tm, D), lambda i, j: (i, 0)),
                  pl.BlockSpec((1, D), lambda i, j: (0, 0)),
                  pl.BlockSpec((D, tn), lambda i, j: (0, j))],
        out_specs=pl.BlockSpec((tm, tn), lambda i, j: (i, j)),
        out_shape=jax.ShapeDtypeStruct((M, N), F32),
        scratch_shapes=[pltpu.VMEM((tm, D), BF16)],
        compiler_params=pltpu.CompilerParams(
            dimension_semantics=("parallel", "arbitrary"), vmem_limit_bytes=VMEM_LIMIT),
        name="in_proj",
    )(x2, nw, w)


def _outproj_body(x_ref, ya_ref, yb_ref, yc_ref, wa_ref, wb_ref, wc_ref, fw_ref, o_ref, *, final):
    acc = x_ref[...]
    acc = acc + _dot(ya_ref[...], wa_ref[...])
    acc = acc + _dot(yb_ref[...], wb_ref[...])
    acc = acc + _dot(yc_ref[...], wc_ref[...])
    if final:
        ms = jnp.mean(acc * acc, axis=-1, keepdims=True)
        acc = acc * lax.rsqrt(ms + NORM_EPS) * fw_ref[...]
    o_ref[...] = acc


def _out_proj(x2, ya, yb, yc, wa, wb, wc, fw, tm, final):
    M, D = x2.shape
    row = lambda width: pl.BlockSpec((tm, width), lambda i: (i, 0))
    full = lambda a: pl.BlockSpec(a.shape, lambda i: (0, 0))
    return pl.pallas_call(
        functools.partial(_outproj_body, final=final),
        grid=(M // tm,),
        in_specs=[row(D), row(ya.shape[1]), row(yb.shape[1]), row(yc.shape[1]),
                  full(wa), full(wb), full(wc), full(fw)],
        out_specs=row(D),
        out_shape=jax.ShapeDtypeStruct((M, D), F32),
        compiler_params=pltpu.CompilerParams(
            dimension_semantics=("parallel",), vmem_limit_bytes=VMEM_LIMIT),
        name="out_proj_final" if final else "out_proj",
    )(x2, ya, yb, yc, wa, wb, wc, fw)


def _lru_body(xa_ref, ga_ref, cw_ref, vec_ref, wa_ref, wx_ref, o_ref, xbuf, hcar):
    tt = xa_ref.shape[1]
    W = xa_ref.shape[2]

    @pl.when(pl.program_id(1) == 0)
    def _():
        xbuf[0:8, :] = jnp.zeros((8, W), F32)
        hcar[...] = jnp.zeros((8, W), F32)

    xa = xa_ref[0]
    xbuf[8:8 + tt, :] = xa
    xc = vec_ref[0:1, :] + cw_ref[LRU_CONV - 1:LRU_CONV, :] * xa
    for s in range(1, LRU_CONV):
        xc = xc + cw_ref[LRU_CONV - 1 - s:LRU_CONV - s, :] * xbuf[8 - s:8 - s + tt, :]
    xbuf[0:8, :] = xa[tt - 8:tt, :]

    r = _sigmoid(_dot(xc, wa_ref[...]) + vec_ref[1:2, :])
    gi = _sigmoid(_dot(xc, wx_ref[...]) + vec_ref[2:3, :])
    lam = vec_ref[3:4, :]
    softplus_neg_lam = jnp.maximum(-lam, 0.0) + jnp.log1p(jnp.exp(-jnp.abs(lam)))
    log_a = (-LRU_C) * r * softplus_neg_lam
    a = jnp.exp(log_a)
    b = jnp.sqrt(-jnp.tanh(log_a) * (a * a + 1.0)) * (gi * xc)

    row = lax.broadcasted_iota(jnp.int32, (tt, 1), 0)
    d = 1
    while d < tt:
        keep = row >= d
        a_sh = pltpu.roll(a, d, 0)
        b_sh = pltpu.roll(b, d, 0)
        b = b + jnp.where(keep, a * b_sh, 0.0)
        a = jnp.where(keep, a * a_sh, a)
        d *= 2
    h = b + a * hcar[0:1, :]
    hcar[...] = jnp.broadcast_to(h[tt - 1:tt, :], (8, W))
    o_ref[0] = h * _silu(ga_ref[0])


def _lru(p3, cw, vec, wa, wx, width, tt):
    B, T, _ = p3.shape
    full = lambda a: pl.BlockSpec(a.shape, lambda b, t: (0, 0))
    return pl.pallas_call(
        _lru_body,
        grid=(B, T // tt),
        in_specs=[pl.BlockSpec((1, tt, width), lambda b, t: (b, t, 0)),
                  pl.BlockSpec((1, tt, width), lambda b, t: (b, t, 1)),
                  full(cw), full(vec), full(wa), full(wx)],
        out_specs=pl.BlockSpec((1, tt, width), lambda b, t: (b, t, 0)),
        out_shape=jax.ShapeDtypeStruct((B, T, width), F32),
        scratch_shapes=[pltpu.VMEM((tt + 8, width), F32), pltpu.VMEM((8, width), F32)],
        compiler_params=pltpu.CompilerParams(
            dimension_semantics=("parallel", "arbitrary"), vmem_limit_bytes=VMEM_LIMIT),
        name="rg_lru",
    )(p3, p3, cw, vec, wa, wx)


def _rwkv_body(r_ref, k_ref, v_ref, wa_ref, g_ref, par_ref, mixwa_ref, wup_ref, aup_ref, o_ref,
               carry, s_ref, at_s, rt_s, kt_s, bt_s, kd_s, bd_s, v_s, gc_s, y_s,
               rq_s, y0_s, g_s, s0_s):
    C = RWKV_CHUNK
    tt = r_ref.shape[1]
    nc = tt // C
    H2 = 2 * C

    @pl.when(pl.program_id(2) == 0)
    def _():
        carry[...] = jnp.zeros(carry.shape, F32)
        s_ref[...] = jnp.zeros(s_ref.shape, F32)

    lane = lax.broadcasted_iota(jnp.int32, (1, LANES), 1)
    head0 = lane < HEAD_DIM
    row = lax.broadcasted_iota(jnp.int32, (tt, 1), 0)

    par = par_ref[...]
    w0, a0, k_k, k_a = par[0:1], par[1:2], par[2:3], par[3:4]
    ln_w, ln_b, r_k = par[4:5], par[5:6], par[6:7]
    mix_r, mix_k, mix_v = par[7:8], par[8:9], par[9:10]

    def shift_lerp(ref, mix, slot):
        s = ref[0]
        prev = jnp.where(row == 0, carry[slot, 0:1, :], pltpu.roll(s, 1, 0))
        carry[slot] = jnp.broadcast_to(s[tt - 1:tt, :], (8, LANES))
        return s + mix * (prev - s)

    def headsum(x):
        s0 = jnp.sum(jnp.where(head0, x, 0.0), axis=-1, keepdims=True)
        s1 = jnp.sum(jnp.where(head0, 0.0, x), axis=-1, keepdims=True)
        return jnp.where(head0, s0, s1)

    r = shift_lerp(r_ref, mix_r, 0)
    k = shift_lerp(k_ref, mix_k, 1)
    v = shift_lerp(v_ref, mix_v, 2)
    wa = shift_lerp(wa_ref, mixwa_ref[...], 3)

    log_w = (-RWKV_DECAY_SCALE) * _sigmoid(w0 + _dot(jnp.tanh(wa), wup_ref[...]))
    a = _sigmoid(a0 + _dot(wa, aup_ref[...]))
    kk = k * k_k
    kk = kk / jnp.maximum(jnp.sqrt(headsum(kk * kk)), 1e-12)
    k = k * (1.0 + (a - 1.0) * k_a)
    be = kk * a

    rc = row & (C - 1)
    c = log_w
    d = 1
    while d < C:
        c = c + jnp.where(rc >= d, pltpu.roll(c, d, 0), 0.0)
        d *= 2
    c_last = jnp.broadcast_to(c.reshape(nc, C, LANES)[:, C - 1:C, :], (nc, C, LANES)).reshape(tt, LANES)
    e_inc = jnp.exp(c)
    e_inv = jnp.exp(-c)
    e_end = jnp.exp(c_last - c)
    at_s[...] = -kk * jnp.exp(c - log_w)
    rt_s[...] = r * e_inc
    kt_s[...] = k * e_inv
    bt_s[...] = be * e_inv
    kd_s[...] = k * e_end
    bd_s[...] = be * e_end
    v_s[...] = v
    gc_s[...] = jnp.exp(c_last)

    ri = lax.broadcasted_iota(jnp.int32, (H2, H2), 0)
    ci = lax.broadcasted_iota(jnp.int32, (H2, H2), 1)
    strict = ri > ci
    incl = ri >= ci
    same16 = (ri >> 4) == (ci >> 4)
    same32 = (ri >> 5) == (ci >> 5)
    eye = (ri == ci).astype(F32)

    def stack(x):
        return jnp.concatenate([jnp.where(head0, x, 0.0), jnp.where(head0, 0.0, x)], axis=0)

    def prepare(ch, _):
        rows = pl.ds(pl.multiple_of(ch * C, C), C)
        out = pl.ds(pl.multiple_of(ch * H2, H2), H2)
        a_st = stack(at_s[rows, :])
        r_st = stack(rt_s[rows, :])
        v_st = stack(v_s[rows, :])
        kd_st = stack(kd_s[rows, :])
        bd_st = stack(bd_s[rows, :])
        left = jnp.concatenate([a_st, r_st], axis=0)
        right = jnp.concatenate([stack(kt_s[rows, :]), stack(bt_s[rows, :])], axis=0)
        p = _dot(left, right, 1, 1)
        a_ak = jnp.where(strict, p[0:H2, 0:H2], 0.0)
        n = jnp.where(strict, p[0:H2, H2:2 * H2], 0.0)
        a_rk = jnp.where(incl, p[H2:2 * H2, 0:H2], 0.0)
        a_rb = jnp.where(incl, p[H2:2 * H2, H2:2 * H2], 0.0)
        n0 = jnp.where(same16, n, 0.0)
        t = eye + n0
        q = n0
        for _ in range(3):
            q = _dot(q, q)
            t = t + _dot(t, q)
        n1 = jnp.where(same32 & jnp.logical_not(same16), n, 0.0)
        t = t + _dot(_dot(t, n1), t)
        n2 = jnp.where(same32, 0.0, n)
        t = t + _dot(_dot(t, n2), t)
        w1 = _dot(a_ak, v_st)
        taw = _dot(t, jnp.concatenate([a_st, w1], axis=1))
        ta = taw[:, 0:LANES]
        tw = taw[:, LANES:2 * LANES]
        ry = _dot(a_rb, taw)
        rq_s[out, :] = r_st + ry[:, 0:LANES]
        y0_s[out, :] = _dot(a_rk, v_st) + ry[:, LANES:2 * LANES]
        g_s[out, :] = _dot(ta, bd_st, 0, 0)
        s0_s[out, :] = _dot(jnp.concatenate([v_st, tw], axis=0),
                            jnp.concatenate([kd_st, bd_st], axis=0), 0, 0)
        return 0

    lax.fori_loop(0, nc, prepare, 0)

    def advance(ch, s):
        rows = pl.ds(pl.multiple_of(ch * C, C), C)
        blk = pl.ds(pl.multiple_of(ch * H2, H2), H2)
        y_st = _dot(rq_s[blk, :], s, 1, 1) + y0_s[blk, :]
        y_s[rows, :] = y_st[0:C, :] + y_st[C:H2, :]
        return s * gc_s[pl.ds(pl.multiple_of(ch * C, C), 1), :] + _dot(s, g_s[blk, :]) + s0_s[blk, :]

    s_ref[...] = lax.fori_loop(0, nc, advance, s_ref[...])

    y = y_s[...]
    mu = headsum(y) * (1.0 / HEAD_DIM)
    yc = y - mu
    var = headsum(yc * yc) * (1.0 / HEAD_DIM)
    yn = yc * lax.rsqrt(var + RWKV_GN_EPS) * ln_w + ln_b
    bonus = headsum(r * k * r_k) * v
    o_ref[0] = (yn + bonus) * _silu(g_ref[0])


def _rwkv(p3, par, mixwa, wup, aup, col0, gcol0, npairs, tt):
    B, T, _ = p3.shape
    C = RWKV_CHUNK
    nc = tt // C
    col = lambda off: pl.BlockSpec((1, tt, LANES), lambda b, j, t: (b, t, off + j))
    fixed = lambda off: pl.BlockSpec((1, tt, LANES), lambda b, j, t: (b, t, off))
    blk = lambda f32_rows: pltpu.VMEM((f32_rows, LANES), F32)
    return pl.pallas_call(
        _rwkv_body,
        grid=(B, npairs, T // tt),
        in_specs=[col(col0), col(col0 + npairs), col(col0 + 2 * npairs), fixed(col0 + 3 * npairs),
                  col(gcol0),
                  pl.BlockSpec((16, LANES), lambda b, j, t: (0, j)),
                  pl.BlockSpec((1, LANES), lambda b, j, t: (0, 0)),
                  pl.BlockSpec((LANES, LANES), lambda b, j, t: (0, j)),
                  pl.BlockSpec((LANES, LANES), lambda b, j, t: (0, j))],
        out_specs=pl.BlockSpec((1, tt, LANES), lambda b, j, t: (b, t, j)),
        out_shape=jax.ShapeDtypeStruct((B, T, npairs * LANES), F32),
        scratch_shapes=[pltpu.VMEM((4, 8, LANES), F32), blk(LANES)]
                       + [blk(tt)] * 9 + [blk(nc * 2 * C)] * 4,
        compiler_params=pltpu.CompilerParams(
            dimension_semantics=("parallel", "parallel", "arbitrary"), vmem_limit_bytes=VMEM_LIMIT),
        name="rwkv7",
    )(p3, p3, p3, p3, p3, par, mixwa, wup, aup)


def _moba_body(q_ref, k_ref, v_ref, g_ref, bown_ref, bprev_ref, bfar_ref, o_ref, kmean_ref):
    BS = MOBA_BLOCK
    T = k_ref.shape[1]
    nb = T // BS
    i = pl.program_id(2)
    lane = lax.broadcasted_iota(jnp.int32, (1, LANES), 1)
    head0 = lane < HEAD_DIM

    @pl.when(i == 0)
    def _():
        kmean_ref[...] = jnp.mean(k_ref[0].reshape(nb, BS, LANES), axis=1)

    q = q_ref[0]
    qh = (jnp.where(head0, q, 0.0), jnp.where(head0, 0.0, q))
    qs = tuple((x * (HEAD_DIM ** -0.5)).astype(BF16) for x in qh)
    kmean = kmean_ref[...]
    blk = lax.broadcasted_iota(jnp.int32, (1, nb), 1).astype(F32)
    i_f = i.astype(F32)

    sel = []
    for h in range(2):
        gate = lax.dot_general(qh[h], kmean, (((1,), (1,)), ((), ())),
                               precision=lax.Precision.HIGHEST, preferred_element_type=F32)
        gate = jnp.where(blk < i_f, gate, -jnp.inf)
        picks = []
        for rnk in range(MOBA_TOPK):
            top = jnp.max(gate, axis=-1, keepdims=True)
            idx = jnp.min(jnp.where(gate == top, blk, float(nb)), axis=-1, keepdims=True)
            gate = jnp.where(blk == idx, -jnp.inf, gate)
            picks.append(jnp.where(rnk < i, idx, -5.0))
        sel.append(picks)

    def chosen(h, j):
        j_f = j.astype(F32)
        return (sel[h][0] == j_f) | (sel[h][1] == j_f) | (sel[h][2] == j_f)

    def block_kv(j):
        rows = pl.ds(pl.multiple_of(j * BS, BS), BS)
        return k_ref[0, rows, :].astype(BF16), v_ref[0, rows, :].astype(BF16)

    kb, vb = block_kv(i)
    state = []
    for h in range(2):
        s = _dot(qs[h], kb, 1, 1) + bown_ref[h]
        m = jnp.max(s, axis=-1, keepdims=True)
        p = jnp.exp(s - m)
        state += [m, jnp.sum(p, axis=-1, keepdims=True), _dot(p, vb)]

    def attend(j, state, bias, valid):
        kb, vb = block_kv(j)
        new = []
        for h in range(2):
            m, l, acc = state[3 * h:3 * h + 3]
            s = _dot(qs[h], kb, 1, 1) + bias(h)
            s = jnp.where(chosen(h, j) & valid, s, -jnp.inf)
            m_new = jnp.maximum(m, jnp.max(s, axis=-1, keepdims=True))
            alpha = jnp.exp(m - m_new)
            p = jnp.exp(s - m_new)
            new += [m_new, alpha * l + jnp.sum(p, axis=-1, keepdims=True), alpha * acc + _dot(p, vb)]
        return new

    state = attend(jnp.maximum(i - 1, 0), state, lambda h: bprev_ref[h], i >= 1)
    state = lax.fori_loop(0, jnp.maximum(i - 1, 0),
                          lambda j, st: tuple(attend(j, list(st), lambda h: bfar_ref[h], True)),
                          tuple(state))
    out = jnp.where(head0, state[2] / state[1], state[5] / state[4])
    o_ref[0] = out * _silu(g_ref[0])


def _moba(p3, bown, bprev, bfar, col0, gcol0, npairs):
    B, T, _ = p3.shape
    BS = MOBA_BLOCK
    nb = T // BS
    return pl.pallas_call(
        _moba_body,
        grid=(B, npairs, nb),
        in_specs=[pl.BlockSpec((1, BS, LANES), lambda b, j, i: (b, i, col0 + j)),
                  pl.BlockSpec((1, T, LANES), lambda b, j, i: (b, 0, col0 + npairs + j)),
                  pl.BlockSpec((1, T, LANES), lambda b, j, i: (b, 0, col0 + 2 * npairs + j)),
                  pl.BlockSpec((1, BS, LANES), lambda b, j, i: (b, i, gcol0 + j)),
                  pl.BlockSpec((2, BS, BS), lambda b, j, i: (j, 0, 0)),
                  pl.BlockSpec((2, BS, BS), lambda b, j, i: (j, 0, 0)),
                  pl.BlockSpec((2, 1, BS), lambda b, j, i: (j, 0, 0))],
        out_specs=pl.BlockSpec((1, BS, LANES), lambda b, j, i: (b, i, j)),
        out_shape=jax.ShapeDtypeStruct((B, T, npairs * LANES), F32),
        scratch_shapes=[pltpu.VMEM((nb, LANES), F32)],
        compiler_params=pltpu.CompilerParams(
            dimension_semantics=("parallel", "parallel", "arbitrary"), vmem_limit_bytes=VMEM_LIMIT),
        name="moba",
    )(p3, p3, p3, p3, bown, bprev, bfar)


def _rel_bucket_of(dist):
    max_exact = REL_BUCKETS // 2
    large = max_exact + (jnp.log(jnp.maximum(dist, 1).astype(F32) / max_exact)
                         / math.log(REL_MAX_DIST / max_exact) * (REL_BUCKETS - max_exact)).astype(jnp.int32)
    large = jnp.minimum(large, REL_BUCKETS - 1)
    return jnp.where(dist < max_exact, dist, large)


def _moba_bias_tables(rel_bias):
    BS = MOBA_BLOCK
    per_dist = rel_bias.astype(F32)[_rel_bucket_of(jnp.arange(2 * BS))]
    qi = np.arange(BS)[:, None]
    ki = np.arange(BS)[None, :]
    own = jnp.where((ki <= qi)[..., None], per_dist[np.maximum(qi - ki, 0)], -jnp.inf)
    prev = per_dist[qi - ki + BS]
    far = jnp.broadcast_to(per_dist[2 * BS - 1][:, None, None], (rel_bias.shape[1], 1, BS))
    return jnp.transpose(own, (2, 0, 1)), jnp.transpose(prev, (2, 0, 1)), far


def _block_diag(w):
    g, n, _ = w.shape
    return jnp.einsum('gij,gh->gihj', w, jnp.eye(g, dtype=w.dtype)).reshape(g * n, g * n)


def kernel(x, norm_w, w_in, w_out, lru_conv_w, lru_conv_b, lru_gate_a_w, lru_gate_a_b, lru_gate_x_w, lru_gate_x_b, lru_lambda, rwkv_mix, rwkv_w0, rwkv_w_up, rwkv_a0, rwkv_a_up, rwkv_k_k, rwkv_k_a, rwkv_r_k, rwkv_ln_w, rwkv_ln_b, rel_bias, final_norm_w):
    B, T, D = x.shape
    depth = w_in.shape[0]
    lru_w = lru_conv_w.shape[2]
    rwkv_w = rwkv_w0.shape[1]
    moba_w = rel_bias.shape[1] * HEAD_DIM
    rwkv_pairs = rwkv_w // LANES
    moba_pairs = moba_w // LANES
    rwkv_col = (2 * lru_w) // LANES
    rwkv_gate_col = rwkv_col + (3 * rwkv_w + 2 * RWKV_LORA) // LANES
    moba_col = rwkv_gate_col + rwkv_w // LANES
    moba_gate_col = moba_col + (3 * moba_w) // LANES
    n_cols = w_in.shape[2]
    assert (moba_gate_col + moba_w // LANES) * LANES == n_cols
    assert T % MOBA_BLOCK == 0 and 2 * RWKV_LORA == LANES

    M = B * T
    tm = 512
    tn = n_cols // 3
    tt_lru = 512
    tt_rwkv = 512

    bown, bprev, bfar = _moba_bias_tables(rel_bias)
    w_in_b = w_in.astype(BF16)
    w_out_b = w_out.astype(BF16)
    zeros_lora = jnp.zeros((RWKV_LORA, rwkv_w), F32)
    fw = final_norm_w.reshape(1, D)

    x2 = x.reshape(M, D)
    for l in range(depth):
        p = _in_proj(x2, norm_w[l].reshape(1, D), w_in_b[l], tm, tn)
        p3 = p.reshape(B, T, n_cols)

        vec = jnp.stack([lru_conv_b[l], lru_gate_a_b[l], lru_gate_x_b[l], lru_lambda[l]]
                        + [jnp.zeros_like(lru_lambda[l])] * 4)
        ya = _lru(p3, lru_conv_w[l], vec, _block_diag(lru_gate_a_w[l]).astype(BF16),
                  _block_diag(lru_gate_x_w[l]).astype(BF16), lru_w, tt_lru)

        mix = rwkv_mix[l]
        par = jnp.stack([rwkv_w0[l], rwkv_a0[l], rwkv_k_k[l], rwkv_k_a[l], rwkv_ln_w[l], rwkv_ln_b[l],
                         rwkv_r_k[l].reshape(rwkv_w), mix[0:rwkv_w], mix[rwkv_w:2 * rwkv_w],
                         mix[2 * rwkv_w:3 * rwkv_w]] + [jnp.zeros((rwkv_w,), F32)] * 6)
        wup = jnp.concatenate([rwkv_w_up[l], zeros_lora], axis=0).astype(BF16)
        aup = jnp.concatenate([zeros_lora, rwkv_a_up[l]], axis=0).astype(BF16)
        yb = _rwkv(p3, par, mix[3 * rwkv_w:].reshape(1, LANES), wup, aup,
                   rwkv_col, rwkv_gate_col, rwkv_pairs, tt_rwkv)

        yc = _moba(p3, bown, bprev, bfar, moba_col, moba_gate_col, moba_pairs)

        wo = w_out_b[l]
        x2 = _out_proj(x2, ya.reshape(M, lru_w), yb.reshape(M, rwkv_w), yc.reshape(M, moba_w),
                       wo[0:lru_w], wo[lru_w:lru_w + rwkv_w], wo[lru_w + rwkv_w:], fw, tm,
                       final=(l == depth - 1))
    return x2.reshape(B, T, D)
```

```python
import functools
import math

import numpy as np
import jax
import jax.numpy as jnp
from jax import lax
from jax.experimental import pallas as pl
from jax.experimental.pallas import tpu as pltpu

F32 = jnp.float32
BF16 = jnp.bfloat16

LANES = 128
HEAD_DIM = 64
NORM_EPS = 1e-6
LRU_CONV = 4
LRU_C = 8.0
RWKV_LORA = 64
RWKV_DECAY_SCALE = math.exp(-0.5)
RWKV_GN_EPS = 64e-5
RWKV_CHUNK = 64
MOBA_BLOCK = 256
MOBA_TOPK = 3
REL_BUCKETS = 32
REL_MAX_DIST = 128
VMEM_LIMIT = 56 * 1024 * 1024


def _dot(a, b, ca=1, cb=0):
    return lax.dot_general(a.astype(BF16), b.astype(BF16), (((ca,), (cb,)), ((), ())),
                           preferred_element_type=F32)


def _sigmoid(x):
    return 1.0 / (1.0 + jnp.exp(-x))


def _silu(x):
    return x * _sigmoid(x)


def _inproj_body(x_ref, nw_ref, w_ref, o_ref, h_ref):
    @pl.when(pl.program_id(1) == 0)
    def _():
        x = x_ref[...]
        ms = jnp.mean(x * x, axis=-1, keepdims=True)
        h_ref[...] = (x * lax.rsqrt(ms + NORM_EPS) * nw_ref[...]).astype(BF16)

    o_ref[...] = jnp.dot(h_ref[...], w_ref[...], preferred_element_type=F32)


def _in_proj(x2, nw, w, tm, tn):
    M, D = x2.shape
    N = w.shape[1]
    return pl.pallas_call(
        _inproj_body,
        grid=(M // tm, N // tn),
        in_specs=[pl.BlockSpec((tm, D), lambda i, j: (i, 0)),
                  pl.BlockSpec((1, D), lambda i, j: (0, 0)),
                  pl.BlockSpec((D, tn), lambda i, j: (0, j))],
        out_specs=pl.BlockSpec((tm, tn), lambda i, j: (i, j)),
        out_shape=jax.ShapeDtypeStruct((M, N), F32),
        scratch_shapes=[pltpu.VMEM((tm, D), BF16)],
        compiler_params=pltpu.CompilerParams(
            dimension_semantics=("parallel", "arbitrary"), vmem_limit_bytes=VMEM_LIMIT),
        name="in_proj",
    )(x2, nw, w)


def _outproj_body(x_ref, ya_ref, yb_ref, yc_ref, wa_ref, wb_ref, wc_ref, fw_ref, o_ref, *, final):
    acc = x_ref[...]
    acc = acc + _dot(ya_ref[...], wa_ref[...])
    acc = acc + _dot(yb_ref[...], wb_ref[...])
    acc = acc + _dot(yc_ref[...], wc_ref[...])
    if final:
        ms = jnp.mean(acc * acc, axis=-1, keepdims=True)
        acc = acc * lax.rsqrt(ms + NORM_EPS) * fw_ref[...]
    o_ref[...] = acc


def _out_proj(x2, ya, yb, yc, wa, wb, wc, fw, tm, final):
    M, D = x2.shape
    row = lambda width: pl.BlockSpec((tm, width), lambda i: (i, 0))
    full = lambda a: pl.BlockSpec(a.shape, lambda i: (0, 0))
    return pl.pallas_call(
        functools.partial(_outproj_body, final=final),
        grid=(M // tm,),
        in_specs=[row(D), row(ya.shape[1]), row(yb.shape[1]), row(yc.shape[1]),
                  full(wa), full(wb), full(wc), full(fw)],
        out_specs=row(D),
        out_shape=jax.ShapeDtypeStruct((M, D), F32),
        compiler_params=pltpu.CompilerParams(
            dimension_semantics=("parallel",), vmem_limit_bytes=VMEM_LIMIT),
        name="out_proj_final" if final else "out_proj",
    )(x2, ya, yb, yc, wa, wb, wc, fw)


def _lru_body(xa_ref, ga_ref, cw_ref, vec_ref, wa_ref, wx_ref, o_ref, xbuf, hcar):
    tt = xa_ref.shape[1]
    W = xa_ref.shape[2]

    @pl.when(pl.program_id(1) == 0)
    def _():
        xbuf[0:8, :] = jnp.zeros((8, W), F32)
        hcar[...] = jnp.zeros((8, W), F32)

    xa = xa_ref[0]
    xbuf[8:8 + tt, :] = xa
    xc = vec_ref[0:1, :] + cw_ref[LRU_CONV - 1:LRU_CONV, :] * xa
    for s in range(1, LRU_CONV):
        xc = xc + cw_ref[LRU_CONV - 1 - s:LRU_CONV - s, :] * xbuf[8 - s:8 - s + tt, :]
    xbuf[0:8, :] = xa[tt - 8:tt, :]

    r = _sigmoid(_dot(xc, wa_ref[...]) + vec_ref[1:2, :])
    gi = _sigmoid(_dot(xc, wx_ref[...]) + vec_ref[2:3, :])
    lam = vec_ref[3:4, :]
    softplus_neg_lam = jnp.maximum(-lam, 0.0) + jnp.log1p(jnp.exp(-jnp.abs(lam)))
    log_a = (-LRU_C) * r * softplus_neg_lam
    a = jnp.exp(log_a)
    b = jnp.sqrt(-jnp.tanh(log_a) * (a * a + 1.0)) * (gi * xc)

    row = lax.broadcasted_iota(jnp.int32, (tt, 1), 0)
    d = 1
    while d < tt:
        keep = row >= d
        a_sh = pltpu.roll(a, d, 0)
        b_sh = pltpu.roll(b, d, 0)
        b = b + jnp.where(keep, a * b_sh, 0.0)
        a = jnp.where(keep, a * a_sh, a)
        d *= 2
    h = b + a * hcar[0:1, :]
    hcar[...] = jnp.broadcast_to(h[tt - 1:tt, :], (8, W))
    o_ref[0] = h * _silu(ga_ref[0])


def _lru(p3, cw, vec, wa, wx, width, tt):
    B, T, _ = p3.shape
    full = lambda a: pl.BlockSpec(a.shape, lambda b, t: (0, 0))
    return pl.pallas_call(
        _lru_body,
        grid=(B, T // tt),
        in_specs=[pl.BlockSpec((1, tt, width), lambda b, t: (b, t, 0)),
                  pl.BlockSpec((1, tt, width), lambda b, t: (b, t, 1)),
                  full(cw), full(vec), full(wa), full(wx)],
        out_specs=pl.BlockSpec((1, tt, width), lambda b, t: (b, t, 0)),
        out_shape=jax.ShapeDtypeStruct((B, T, width), F32),
        scratch_shapes=[pltpu.VMEM((tt + 8, width), F32), pltpu.VMEM((8, width), F32)],
        compiler_params=pltpu.CompilerParams(
            dimension_semantics=("parallel", "arbitrary"), vmem_limit_bytes=VMEM_LIMIT),
        name="rg_lru",
    )(p3, p3, cw, vec, wa, wx)


def _rwkv_body(r_ref, k_ref, v_ref, wa_ref, g_ref, par_ref, mixwa_ref, wup_ref, aup_ref, o_ref,
               carry, s_ref):
    C = RWKV_CHUNK
    tt = r_ref.shape[1]
    nc = tt // C
    H2 = 2 * C

    @pl.when(pl.program_id(2) == 0)
    def _():
        carry[...] = jnp.zeros(carry.shape, F32)
        s_ref[...] = jnp.zeros(s_ref.shape, F32)

    lane = lax.broadcasted_iota(jnp.int32, (1, LANES), 1)
    head0 = lane < HEAD_DIM
    row = lax.broadcasted_iota(jnp.int32, (tt, 1), 0)

    par = par_ref[...]
    w0, a0, k_k, k_a = par[0:1], par[1:2], par[2:3], par[3:4]
    ln_w, ln_b, r_k = par[4:5], par[5:6], par[6:7]
    mix_r, mix_k, mix_v = par[7:8], par[8:9], par[9:10]

    def shift_lerp(ref, mix, slot):
        s = ref[0]
        prev = jnp.where(row == 0, carry[slot, 0:1, :], pltpu.roll(s, 1, 0))
        carry[slot] = jnp.broadcast_to(s[tt - 1:tt, :], (8, LANES))
        return s + mix * (prev - s)

    def headsum(x):
        s0 = jnp.sum(jnp.where(head0, x, 0.0), axis=-1, keepdims=True)
        s1 = jnp.sum(jnp.where(head0, 0.0, x), axis=-1, keepdims=True)
        return jnp.where(head0, s0, s1)

    r = shift_lerp(r_ref, mix_r, 0)
    k = shift_lerp(k_ref, mix_k, 1)
    v = shift_lerp(v_ref, mix_v, 2)
    wa = shift_lerp(wa_ref, mixwa_ref[...], 3)

    log_w = (-RWKV_DECAY_SCALE) * _sigmoid(w0 + _dot(jnp.tanh(wa), wup_ref[...]))
    a = _sigmoid(a0 + _dot(wa, aup_ref[...]))
    kk = k * k_k
    kk = kk / jnp.maximum(jnp.sqrt(headsum(kk * kk)), 1e-12)
    k = k * (1.0 + (a - 1.0) * k_a)
    be = kk * a

    rc = row & (C - 1)
    c = log_w
    d = 1
    while d < C:
        c = c + jnp.where(rc >= d, pltpu.roll(c, d, 0), 0.0)
        d *= 2
    c_last = jnp.broadcast_to(c.reshape(nc, C, LANES)[:, C - 1:C, :], (nc, C, LANES)).reshape(tt, LANES)
    e_inc = jnp.exp(c)
    e_inv = jnp.exp(-c)
    e_end = jnp.exp(c_last - c)
    gamma_end = jnp.exp(c_last)

    ri = lax.broadcasted_iota(jnp.int32, (H2, H2), 0)
    ci = lax.broadcasted_iota(jnp.int32, (H2, H2), 1)
    strict = ri > ci
    incl = ri >= ci
    same16 = (ri >> 4) == (ci >> 4)
    same32 = (ri >> 5) == (ci >> 5)
    eye = (ri == ci).astype(F32)

    def stacked(x):
        lo = jnp.where(head0, x, 0.0)
        hi = jnp.where(head0, 0.0, x)
        return [jnp.concatenate([lo[ch * C:(ch + 1) * C], hi[ch * C:(ch + 1) * C]], axis=0) for ch in range(nc)]

    def each(f, *lists):
        return [f(*xs) for xs in zip(*lists)]

    a_st = stacked(-kk * jnp.exp(c - log_w))
    r_st = stacked(r * e_inc)
    k_st = stacked(k * e_inv)
    b_st = stacked(be * e_inv)
    kd_st = stacked(k * e_end)
    bd_st = stacked(be * e_end)
    v_st = stacked(v)

    p = each(lambda a, rr, kt, bt: _dot(jnp.concatenate([a, rr], axis=0),
                                        jnp.concatenate([kt, bt], axis=0), 1, 1), a_st, r_st, k_st, b_st)
    a_ak = each(lambda x: jnp.where(strict, x[0:H2, 0:H2], 0.0), p)
    n = each(lambda x: jnp.where(strict, x[0:H2, H2:2 * H2], 0.0), p)
    a_rk = each(lambda x: jnp.where(incl, x[H2:2 * H2, 0:H2], 0.0), p)
    a_rb = each(lambda x: jnp.where(incl, x[H2:2 * H2, H2:2 * H2], 0.0), p)
    q = each(lambda x: jnp.where(same16, x, 0.0), n)
    t = each(lambda x: eye + x, q)
    for _ in range(3):
        q = each(lambda x: _dot(x, x), q)
        t = each(lambda x, y: x + _dot(x, y), t, q)
    tn = each(lambda x, y: _dot(x, jnp.where(same32 & jnp.logical_not(same16), y, 0.0)), t, n)
    t = each(lambda x, y: x + _dot(y, x), t, tn)
    tn = each(lambda x, y: _dot(x, jnp.where(same32, 0.0, y)), t, n)
    t = each(lambda x, y: x + _dot(y, x), t, tn)
    w1 = each(_dot, a_ak, v_st)
    taw = each(lambda x, a, w: _dot(x, jnp.concatenate([a, w], axis=1)), t, a_st, w1)
    ry = each(_dot, a_rb, taw)
    rq = each(lambda rr, x: rr + x[:, 0:LANES], r_st, ry)
    y0 = each(lambda ark, vv, x: _dot(ark, vv) + x[:, LANES:2 * LANES], a_rk, v_st, ry)
    g = each(lambda x, bd: _dot(x[:, 0:LANES], bd, 0, 0), taw, bd_st)
    s0 = each(lambda vv, x, kd, bd: _dot(jnp.concatenate([vv, x[:, LANES:2 * LANES]], axis=0),
                                         jnp.concatenate([kd, bd], axis=0), 0, 0), v_st, taw, kd_st, bd_st)

    s = s_ref[...]
    ys = []
    for ch in range(nc):
        y_st = _dot(rq[ch], s, 1, 1) + y0[ch]
        ys.append(y_st[0:C, :] + y_st[C:H2, :])
        s = s * gamma_end[ch * C:ch * C + 1, :] + _dot(s, g[ch]) + s0[ch]
    s_ref[...] = s

    y = jnp.concatenate(ys, axis=0)
    mu = headsum(y) * (1.0 / HEAD_DIM)
    yc = y - mu
    var = headsum(yc * yc) * (1.0 / HEAD_DIM)
    yn = yc * lax.rsqrt(var + RWKV_GN_EPS) * ln_w + ln_b
    bonus = headsum(r * k * r_k) * v
    o_ref[0] = (yn + bonus) * _silu(g_ref[0])


def _rwkv(p3, par, mixwa, wup, aup, col0, gcol0, npairs, tt):
    B, T, _ = p3.shape
    col = lambda off: pl.BlockSpec((1, tt, LANES), lambda b, j, t: (b, t, off + j))
    fixed = lambda off: pl.BlockSpec((1, tt, LANES), lambda b, j, t: (b, t, off))
    return pl.pallas_call(
        _rwkv_body,
        grid=(B, npairs, T // tt),
        in_specs=[col(col0), col(col0 + npairs), col(col0 + 2 * npairs), fixed(col0 + 3 * npairs),
                  col(gcol0),
                  pl.BlockSpec((16, LANES), lambda b, j, t: (0, j)),
                  pl.BlockSpec((1, LANES), lambda b, j, t: (0, 0)),
                  pl.BlockSpec((LANES, LANES), lambda b, j, t: (0, j)),
                  pl.BlockSpec((LANES, LANES), lambda b, j, t: (0, j))],
        out_specs=pl.BlockSpec((1, tt, LANES), lambda b, j, t: (b, t, j)),
        out_shape=jax.ShapeDtypeStruct((B, T, npairs * LANES), F32),
        scratch_shapes=[pltpu.VMEM((4, 8, LANES), F32), pltpu.VMEM((LANES, LANES), F32)],
        compiler_params=pltpu.CompilerParams(
            dimension_semantics=("parallel", "parallel", "arbitrary"), vmem_limit_bytes=VMEM_LIMIT),
        name="rwkv7",
    )(p3, p3, p3, p3, p3, par, mixwa, wup, aup)


def _moba_body(q_ref, k_ref, v_ref, g_ref, bown_ref, bprev_ref, o_ref, kmean_ref, kb_ref, vt_ref, acc_ref):
    BS = MOBA_BLOCK
    NB, T = k_ref.shape[0], k_ref.shape[1]
    nb = T // BS
    i = pl.program_id(1)
    lane = lax.broadcasted_iota(jnp.int32, (1, LANES), 1)
    head0 = lane < HEAD_DIM
    streams = [(b, h) for b in range(NB) for h in range(2)]

    def each(f, *lists):
        return [f(*xs) for xs in zip(*lists)]

    @pl.when(i == 0)
    def _():
        for b in range(NB):
            kmean_ref[b] = jnp.mean(k_ref[b].reshape(nb, BS, LANES), axis=1)

        def fill(j, _):
            rows = pl.ds(pl.multiple_of(j * BS, BS), BS)
            for b in range(NB):
                kb_ref[b, rows, :] = k_ref[b, rows, :].astype(BF16)
                vt_ref[b, j] = v_ref[b, rows, :].T.astype(BF16)
            return 0

        lax.fori_loop(0, nb, fill, 0)

    qh = []
    for b in range(NB):
        q = q_ref[b]
        qh += [jnp.where(head0, q, 0.0), jnp.where(head0, 0.0, q)]
    qs = [(x * (HEAD_DIM ** -0.5)).astype(BF16) for x in qh]
    blk = lax.broadcasted_iota(jnp.int32, (nb, 1), 0).astype(F32)
    i_f = i.astype(F32)

    gate = [lax.dot_general(kmean_ref[b], x, (((1,), (1,)), ((), ())),
                            precision=lax.Precision.HIGHEST, preferred_element_type=F32)
            for (b, _), x in zip(streams, qh)]
    gate = each(lambda x: jnp.where(blk < i_f, x, -jnp.inf), gate)
    sel = []
    for rnk in range(MOBA_TOPK):
        top = each(lambda x: jnp.max(x, axis=0, keepdims=True), gate)
        idx = each(lambda x, tp: jnp.min(jnp.where(x == tp, blk, float(nb)), axis=0, keepdims=True), gate, top)
        gate = each(lambda x, ix: jnp.where(blk == ix, -jnp.inf, x), gate, idx)
        sel.append(each(lambda ix: jnp.where(rnk < i, ix, -5.0), idx))

    def chosen(s, j):
        j_f = j.astype(F32)
        return (sel[0][s] == j_f) | (sel[1][s] == j_f) | (sel[2][s] == j_f)

    def scores(j):
        rows = pl.ds(pl.multiple_of(j * BS, BS), BS)
        return [_dot(kb_ref[b, rows, :], qs[s], 1, 1) for s, (b, _) in enumerate(streams)]

    def weighted_values(j, p):
        return [_dot(vt_ref[b, j, h * HEAD_DIM:(h + 1) * HEAD_DIM, :], p[s]) for s, (b, h) in enumerate(streams)]

    def acc_rows(s):
        b, h = streams[s]
        return b, slice(h * HEAD_DIM, (h + 1) * HEAD_DIM)

    s_own = each(lambda x, bh: x + bown_ref[bh[1]], scores(i), streams)
    m = each(lambda x: jnp.max(x, axis=0, keepdims=True), s_own)
    p = each(lambda x, mm: jnp.exp(x - mm), s_own, m)
    l = each(lambda x: jnp.sum(x, axis=0, keepdims=True), p)
    for s, pv in enumerate(weighted_values(i, p)):
        b, rows = acc_rows(s)
        acc_ref[b, rows, :] = pv

    def attend(j, m, l, bias, valid):
        sc = scores(j)
        if bias is not None:
            sc = each(lambda x, bh: x + bias[bh[1]], sc, streams)
        take = [chosen(s, j) & valid for s in range(len(streams))]
        top = each(lambda x: jnp.max(x, axis=0, keepdims=True), sc)
        m_new = each(lambda mm, tk, tp: jnp.maximum(mm, jnp.where(tk, tp, -jnp.inf)), m, take, top)
        p = each(lambda x, tk, mn: jnp.exp(x - jnp.where(tk, mn, jnp.inf)), sc, take, m_new)
        alpha = each(lambda mm, mn: jnp.exp(mm - mn), m, m_new)
        l_new = each(lambda al, ll, x: al * ll + jnp.sum(x, axis=0, keepdims=True), alpha, l, p)
        for s, pv in enumerate(weighted_values(j, p)):
            b, rows = acc_rows(s)
            acc_ref[b, rows, :] = alpha[s] * acc_ref[b, rows, :] + pv
        return m_new, l_new

    m, l = attend(jnp.maximum(i - 1, 0), m, l, bprev_ref, i >= 1)
    m, l = lax.fori_loop(0, jnp.maximum(i - 1, 0),
                         lambda j, st: tuple(tuple(x) for x in attend(j, list(st[0]), list(st[1]), None, True)),
                         (tuple(m), tuple(l)))
    for b in range(NB):
        l_rows = jnp.concatenate([jnp.broadcast_to(l[2 * b], (HEAD_DIM, BS)),
                                  jnp.broadcast_to(l[2 * b + 1], (HEAD_DIM, BS))], axis=0)
        out = (acc_ref[b] / l_rows).T
        o_ref[b] = out * _silu(g_ref[b])


def _moba(p3, bown, bprev, col0, gcol0, npairs):
    B, T, _ = p3.shape
    BS = MOBA_BLOCK
    nb = T // BS
    return pl.pallas_call(
        _moba_body,
        grid=(npairs, nb),
        in_specs=[pl.BlockSpec((B, BS, LANES), lambda j, i: (0, i, col0 + j)),
                  pl.BlockSpec((B, T, LANES), lambda j, i: (0, 0, col0 + npairs + j)),
                  pl.BlockSpec((B, T, LANES), lambda j, i: (0, 0, col0 + 2 * npairs + j)),
                  pl.BlockSpec((B, BS, LANES), lambda j, i: (0, i, gcol0 + j)),
                  pl.BlockSpec((2, BS, BS), lambda j, i: (j, 0, 0)),
                  pl.BlockSpec((2, BS, BS), lambda j, i: (j, 0, 0))],
        out_specs=pl.BlockSpec((B, BS, LANES), lambda j, i: (0, i, j)),
        out_shape=jax.ShapeDtypeStruct((B, T, npairs * LANES), F32),
        scratch_shapes=[pltpu.VMEM((B, nb, LANES), F32), pltpu.VMEM((B, T, LANES), BF16),
                        pltpu.VMEM((B, nb, LANES, BS), BF16), pltpu.VMEM((B, LANES, BS), F32)],
        compiler_params=pltpu.CompilerParams(
            dimension_semantics=("parallel", "arbitrary"), vmem_limit_bytes=VMEM_LIMIT),
        name="moba",
    )(p3, p3, p3, p3, bown, bprev)


def _rel_bucket_of(dist):
    max_exact = REL_BUCKETS // 2
    large = max_exact + (jnp.log(jnp.maximum(dist, 1).astype(F32) / max_exact)
                         / math.log(REL_MAX_DIST / max_exact) * (REL_BUCKETS - max_exact)).astype(jnp.int32)
    large = jnp.minimum(large, REL_BUCKETS - 1)
    return jnp.where(dist < max_exact, dist, large)


def _toeplitz_kq(tab):
    H, L = tab.shape
    BS = L // 2
    flat = jnp.tile(jnp.roll(tab, -1, axis=1), (1, BS))[:, :BS * (L - 1)]
    return flat.reshape(H, BS, L - 1)[:, :, BS - 1:2 * BS - 1]


def _moba_bias_tables(rel_bias):
    BS = MOBA_BLOCK
    per_dist = rel_bias.astype(F32)[_rel_bucket_of(jnp.arange(2 * BS))].T
    per_dist = per_dist - per_dist[:, 2 * BS - 1:2 * BS]
    prev = _toeplitz_kq(per_dist)
    ki = np.arange(BS)[:, None]
    qi = np.arange(BS)[None, :]
    own = jnp.where(ki <= qi, _toeplitz_kq(jnp.roll(per_dist, BS, axis=1)), -jnp.inf)
    return own, prev


def _block_diag(w):
    g, n, _ = w.shape
    return jnp.einsum('gij,gh->gihj', w, jnp.eye(g, dtype=w.dtype)).reshape(g * n, g * n)


def kernel(x, norm_w, w_in, w_out, lru_conv_w, lru_conv_b, lru_gate_a_w, lru_gate_a_b, lru_gate_x_w, lru_gate_x_b, lru_lambda, rwkv_mix, rwkv_w0, rwkv_w_up, rwkv_a0, rwkv_a_up, rwkv_k_k, rwkv_k_a, rwkv_r_k, rwkv_ln_w, rwkv_ln_b, rel_bias, final_norm_w):
    B, T, D = x.shape
    depth = w_in.shape[0]
    lru_w = lru_conv_w.shape[2]
    rwkv_w = rwkv_w0.shape[1]
    moba_w = rel_bias.shape[1] * HEAD_DIM
    rwkv_pairs = rwkv_w // LANES
    moba_pairs = moba_w // LANES
    rwkv_col = (2 * lru_w) // LANES
    rwkv_gate_col = rwkv_col + (3 * rwkv_w + 2 * RWKV_LORA) // LANES
    moba_col = rwkv_gate_col + rwkv_w // LANES
    moba_gate_col = moba_col + (3 * moba_w) // LANES
    n_cols = w_in.shape[2]
    assert (moba_gate_col + moba_w // LANES) * LANES == n_cols
    assert T % MOBA_BLOCK == 0 and 2 * RWKV_LORA == LANES

    M = B * T
    tm = 512
    tn = n_cols // 3
    tt_lru = 512
    tt_rwkv = 512

    bown, bprev = _moba_bias_tables(rel_bias)
    w_in_b = w_in.astype(BF16)
    w_out_b = w_out.astype(BF16)
    zeros_lora = jnp.zeros((RWKV_LORA, rwkv_w), F32)
    fw = final_norm_w.reshape(1, D)

    x2 = x.reshape(M, D)
    for l in range(depth):
        p = _in_proj(x2, norm_w[l].reshape(1, D), w_in_b[l], tm, tn)
        p3 = p.reshape(B, T, n_cols)

        vec = jnp.stack([lru_conv_b[l], lru_gate_a_b[l], lru_gate_x_b[l], lru_lambda[l]]
                        + [jnp.zeros_like(lru_lambda[l])] * 4)
        ya = _lru(p3, lru_conv_w[l], vec, _block_diag(lru_gate_a_w[l]).astype(BF16),
                  _block_diag(lru_gate_x_w[l]).astype(BF16), lru_w, tt_lru)

        mix = rwkv_mix[l]
        par = jnp.stack([rwkv_w0[l], rwkv_a0[l], rwkv_k_k[l], rwkv_k_a[l], rwkv_ln_w[l], rwkv_ln_b[l],
                         rwkv_r_k[l].reshape(rwkv_w), mix[0:rwkv_w], mix[rwkv_w:2 * rwkv_w],
                         mix[2 * rwkv_w:3 * rwkv_w]] + [jnp.zeros((rwkv_w,), F32)] * 6)
        wup = jnp.concatenate([rwkv_w_up[l], zeros_lora], axis=0).astype(BF16)
        aup = jnp.concatenate([zeros_lora, rwkv_a_up[l]], axis=0).astype(BF16)
        yb = _rwkv(p3, par, mix[3 * rwkv_w:].reshape(1, LANES), wup, aup,
                   rwkv_col, rwkv_gate_col, rwkv_pairs, tt_rwkv)

        yc = _moba(p3, bown, bprev, moba_col, moba_gate_col, moba_pairs)

        wo = w_out_b[l]
        x2 = _out_proj(x2, ya.reshape(M, lru_w), yb.reshape(M, rwkv_w), yc.reshape(M, moba_w),
                       wo[0:lru_w], wo[lru_w:lru_w + rwkv_w], wo[lru_w + rwkv_w:], fw, tm,
                       final=(l == depth - 1))
    return x2.reshape(B, T, D)
```

```python
import functools
import math

import numpy as np
import jax
import jax.numpy as jnp
from jax import lax
from jax.experimental import pallas as pl
from jax.experimental.pallas import tpu as pltpu

F32 = jnp.float32
BF16 = jnp.bfloat16

LANES = 128
HEAD_DIM = 64
NORM_EPS = 1e-6
LRU_CONV = 4
LRU_C = 8.0
RWKV_LORA = 64
RWKV_DECAY_SCALE = math.exp(-0.5)
RWKV_GN_EPS = 64e-5
RWKV_CHUNK = 64
MOBA_BLOCK = 256
MOBA_TOPK = 3
REL_BUCKETS = 32
REL_MAX_DIST = 128
LOG2_E = math.log2(math.e)
VMEM_LIMIT = 56 * 1024 * 1024


def _dot(a, b, ca=1, cb=0):
    return lax.dot_general(a.astype(BF16), b.astype(BF16), (((ca,), (cb,)), ((), ())),
                           preferred_element_type=F32)


def _sigmoid(x):
    return 0.5 * jnp.tanh(0.5 * x) + 0.5


def _silu(x):
    return x * _sigmoid(x)


def _rms_norm(x, w):
    ms = jnp.mean(x * x, axis=-1, keepdims=True)
    return x * lax.rsqrt(ms + NORM_EPS) * w


def _norm_body(x_ref, nw_ref, h_ref):
    h_ref[...] = _rms_norm(x_ref[...], nw_ref[...]).astype(BF16)


def _norm_cast(x2, nw, tm):
    M, D = x2.shape
    return pl.pallas_call(
        _norm_body,
        grid=(M // tm,),
        in_specs=[pl.BlockSpec((tm, D), lambda i: (i, 0)), pl.BlockSpec((1, D), lambda i: (0, 0))],
        out_specs=pl.BlockSpec((tm, D), lambda i: (i, 0)),
        out_shape=jax.ShapeDtypeStruct((M, D), BF16),
        compiler_params=pltpu.CompilerParams(dimension_semantics=("parallel",), vmem_limit_bytes=VMEM_LIMIT),
        name="norm_cast",
    )(x2, nw)


def _inproj_body(h_ref, w_ref, o_ref):
    o_ref[...] = jnp.dot(h_ref[...], w_ref[...], preferred_element_type=F32)


def _in_proj(h, w, tm, tn):
    M, D = h.shape
    N = w.shape[1]
    return pl.pallas_call(
        _inproj_body,
        grid=(N // tn, M // tm),
        in_specs=[pl.BlockSpec((tm, D), lambda j, i: (i, 0)),
                  pl.BlockSpec((D, tn), lambda j, i: (0, j))],
        out_specs=pl.BlockSpec((tm, tn), lambda j, i: (i, j)),
        out_shape=jax.ShapeDtypeStruct((M, N), F32),
        compiler_params=pltpu.CompilerParams(
            dimension_semantics=("parallel", "parallel"), vmem_limit_bytes=VMEM_LIMIT),
        name="in_proj",
    )(h, w)


def _outproj_body(x_ref, ya_ref, yb_ref, yc_ref, w_ref, nw_ref, *out_refs, final):
    wa, wb = ya_ref.shape[1], yb_ref.shape[1]
    acc = x_ref[...]
    acc = acc + _dot(ya_ref[...], w_ref[0:wa, :])
    acc = acc + _dot(yb_ref[...], w_ref[wa:wa + wb, :])
    acc = acc + _dot(yc_ref[...], w_ref[wa + wb:, :])
    normed = _rms_norm(acc, nw_ref[...])
    if final:
        out_refs[0][...] = normed
    else:
        out_refs[0][...] = acc
        out_refs[1][...] = normed.astype(BF16)


def _out_proj(x2, ya, yb, yc, w, nw, tm, final):
    M, D = x2.shape
    row = lambda width: pl.BlockSpec((tm, width), lambda i: (i, 0))
    full = lambda a: pl.BlockSpec(a.shape, lambda i: (0, 0))
    if final:
        out_specs, out_shape = row(D), jax.ShapeDtypeStruct((M, D), F32)
    else:
        out_specs = [row(D), row(D)]
        out_shape = [jax.ShapeDtypeStruct((M, D), F32), jax.ShapeDtypeStruct((M, D), BF16)]
    return pl.pallas_call(
        functools.partial(_outproj_body, final=final),
        grid=(M // tm,),
        in_specs=[row(D), row(ya.shape[1]), row(yb.shape[1]), row(yc.shape[1]), full(w), full(nw)],
        out_specs=out_specs,
        out_shape=out_shape,
        compiler_params=pltpu.CompilerParams(
            dimension_semantics=("parallel",), vmem_limit_bytes=VMEM_LIMIT),
        name="out_proj_final" if final else "out_proj",
    )(x2, ya, yb, yc, w, nw)


def _lru_body(xa_ref, ga_ref, cw_ref, vec_ref, wa_ref, wx_ref, o_ref, xbuf, hcar):
    tt = xa_ref.shape[1]
    W = xa_ref.shape[2]

    @pl.when(pl.program_id(1) == 0)
    def _():
        xbuf[0:8, :] = jnp.zeros((8, W), F32)
        hcar[...] = jnp.zeros((8, W), F32)

    xa = xa_ref[0]
    xbuf[8:8 + tt, :] = xa
    xc = vec_ref[0:1, :] + cw_ref[LRU_CONV - 1:LRU_CONV, :] * xa
    for s in range(1, LRU_CONV):
        xc = xc + cw_ref[LRU_CONV - 1 - s:LRU_CONV - s, :] * xbuf[8 - s:8 - s + tt, :]
    xbuf[0:8, :] = xa[tt - 8:tt, :]

    r = _sigmoid(_dot(xc, wa_ref[...]) + vec_ref[1:2, :])
    gi = _sigmoid(_dot(xc, wx_ref[...]) + vec_ref[2:3, :])
    lam = vec_ref[3:4, :]
    softplus_neg_lam = jnp.maximum(-lam, 0.0) + jnp.log1p(jnp.exp(-jnp.abs(lam)))
    log_a = (-LRU_C) * r * softplus_neg_lam
    a = jnp.exp(log_a)
    b = jnp.sqrt(-jnp.tanh(log_a) * (a * a + 1.0)) * (gi * xc)

    row = lax.broadcasted_iota(jnp.int32, (tt, 1), 0)
    d = 1
    while d < tt:
        keep = row >= d
        a_sh = pltpu.roll(a, d, 0)
        b_sh = pltpu.roll(b, d, 0)
        b = b + jnp.where(keep, a * b_sh, 0.0)
        a = jnp.where(keep, a * a_sh, a)
        d *= 2
    h = b + a * hcar[0:1, :]
    hcar[...] = jnp.broadcast_to(h[tt - 1:tt, :], (8, W))
    o_ref[0] = (h * _silu(ga_ref[0])).astype(o_ref.dtype)


def _lru(p3, cw, vec, wa, wx, width, tt):
    B, T, _ = p3.shape
    full = lambda a: pl.BlockSpec(a.shape, lambda b, t: (0, 0))
    return pl.pallas_call(
        _lru_body,
        grid=(B, T // tt),
        in_specs=[pl.BlockSpec((1, tt, width), lambda b, t: (b, t, 0)),
                  pl.BlockSpec((1, tt, width), lambda b, t: (b, t, 1)),
                  full(cw), full(vec), full(wa), full(wx)],
        out_specs=pl.BlockSpec((1, tt, width), lambda b, t: (b, t, 0)),
        out_shape=jax.ShapeDtypeStruct((B, T, width), BF16),
        scratch_shapes=[pltpu.VMEM((tt + 8, width), F32), pltpu.VMEM((8, width), F32)],
        compiler_params=pltpu.CompilerParams(
            dimension_semantics=("parallel", "arbitrary"), vmem_limit_bytes=VMEM_LIMIT),
        name="rg_lru",
    )(p3, p3, cw, vec, wa, wx)


def _rwkv_body(r_ref, k_ref, v_ref, wa_ref, g_ref, par_ref, mixwa_ref, wup_ref, aup_ref, o_ref,
               xs_ref, s_ref):
    C = RWKV_CHUNK
    NB, tt = r_ref.shape[0], r_ref.shape[1]
    nc = tt // C

    @pl.when(pl.program_id(1) == 0)
    def _():
        xs_ref[:, 0:8, :] = jnp.zeros((xs_ref.shape[0], 8, LANES), F32)
        s_ref[...] = jnp.zeros(s_ref.shape, F32)

    lane = lax.broadcasted_iota(jnp.int32, (1, LANES), 1)
    head0 = lane < HEAD_DIM
    rc = lax.broadcasted_iota(jnp.int32, (tt, 1), 0) & (C - 1)

    par = par_ref[...]
    w0, a0, k_k, k_a = par[0:1], par[1:2], par[2:3], par[3:4]
    ln_w, ln_b, r_k = par[4:5], par[5:6], par[6:7]
    mix_r, mix_k, mix_v = par[7:8], par[8:9], par[9:10]

    def shift_lerp(ref, b, mix, slot):
        s = ref[b]
        xs_ref[4 * b + slot, 8:8 + tt, :] = s
        prev = xs_ref[4 * b + slot, 7:7 + tt, :]
        xs_ref[4 * b + slot, 0:8, :] = s[tt - 8:tt, :]
        return s + mix * (prev - s)

    def headsum(x):
        s0 = jnp.sum(jnp.where(head0, x, 0.0), axis=-1, keepdims=True)
        s1 = jnp.sum(jnp.where(head0, 0.0, x), axis=-1, keepdims=True)
        return jnp.where(head0, s0, s1)

    def chunks(x):
        return [x[ch * C:(ch + 1) * C] for ch in range(nc)]

    def front(b):
        r = shift_lerp(r_ref, b, mix_r, 0)
        k = shift_lerp(k_ref, b, mix_k, 1)
        v = shift_lerp(v_ref, b, mix_v, 2)
        wa = shift_lerp(wa_ref, b, mixwa_ref[...], 3)
        log_w = (-RWKV_DECAY_SCALE) * _sigmoid(w0 + _dot(jnp.tanh(wa), wup_ref[...]))
        a = _sigmoid(a0 + _dot(wa, aup_ref[...]))
        kk = k * k_k
        kk = kk * lax.rsqrt(jnp.maximum(headsum(kk * kk), 1e-24))
        k = k * (1.0 + (a - 1.0) * k_a)
        be = kk * a
        c = log_w
        d = 1
        while d < C:
            c = c + jnp.where(rc >= d, pltpu.roll(c, d, 0), 0.0)
            d *= 2
        c_last = jnp.broadcast_to(c.reshape(nc, C, LANES)[:, C - 1:C, :], (nc, C, LANES)).reshape(tt, LANES)
        e_inc = jnp.exp(c)
        e_inv = jnp.exp(-c)
        e_end = jnp.exp(c_last - c)
        parts = dict(a=-kk * jnp.exp(c - log_w), r=r * e_inc, kt=k * e_inv, bt=be * e_inv,
                     kd=k * e_end, bd=be * e_end, v=v, gamma_end=jnp.exp(c_last))
        return {name: chunks(val) for name, val in parts.items()}, headsum(r * k * r_k) * v

    fronts = [front(b) for b in range(NB)]
    gather = lambda name: [x for f, _ in fronts for x in f[name]]

    ti = lax.broadcasted_iota(jnp.int32, (C, LANES), 0)
    si = lax.broadcasted_iota(jnp.int32, (C, LANES), 1) & (C - 1)
    strict = ti > si
    incl = ti >= si
    same16 = (ti >> 4) == (si >> 4)
    same32 = (ti >> 5) == (si >> 5)
    eye = (ti == si).astype(F32)
    same_head = ((lax.broadcasted_iota(jnp.int32, (LANES, LANES), 0) >> 6)
                 == (lax.broadcasted_iota(jnp.int32, (LANES, LANES), 1) >> 6))
    head0_b = jnp.broadcast_to(head0, (C, LANES))

    def embed(x):
        xb = x.astype(BF16)
        zero = jnp.zeros_like(xb)
        return jnp.concatenate([jnp.where(head0_b, xb, zero), jnp.where(head0_b, zero, xb)], axis=0)

    def each(f, *lists):
        return [f(*xs) for xs in zip(*lists)]

    lo = lambda x: x[:, 0:LANES]
    hi = lambda x: x[:, LANES:2 * LANES]
    side = lambda x, y: jnp.concatenate([x, y], axis=1)

    a_c = gather('a')
    r_c = gather('r')
    k_e = each(embed, gather('kt'))
    b_e = each(embed, gather('bt'))
    kd_c = gather('kd')
    bd_c = gather('bd')
    v_c = gather('v')
    v_e = each(embed, v_c)
    gamma_end = gather('gamma_end')

    p = each(lambda aa, rr, kt, bt: _dot(jnp.concatenate([aa, rr], axis=0),
                                         jnp.concatenate([kt, bt], axis=0), 1, 1), a_c, r_c, k_e, b_e)
    a_ak = each(lambda x: jnp.where(strict, x[0:C, 0:LANES], 0.0), p)
    n = each(lambda x: jnp.where(strict, x[0:C, LANES:2 * LANES], 0.0), p)
    a_rk = each(lambda x: jnp.where(incl, x[C:2 * C, 0:LANES], 0.0), p)
    a_rb = each(lambda x: jnp.where(incl, x[C:2 * C, LANES:2 * LANES], 0.0), p)
    q = each(lambda x: jnp.where(same16, x, 0.0), n)
    t = each(lambda x: eye + x, q)
    q = each(lambda x: _dot(x, embed(x)), q)
    for _ in range(2):
        qt = each(lambda x, y: _dot(x, side(embed(y), embed(x))), q, t)
        t = each(lambda x, y: x + lo(y), t, qt)
        q = each(hi, qt)
    t = each(lambda x, y: x + _dot(y, embed(x)), t, q)
    tn = each(lambda x, y: _dot(x, embed(jnp.where(same32 & jnp.logical_not(same16), y, 0.0))), t, n)
    t = each(lambda x, y: x + _dot(y, embed(x)), t, tn)
    tn = each(lambda x, y: _dot(x, embed(jnp.where(same32, 0.0, y))), t, n)
    t = each(lambda x, y: x + _dot(y, embed(x)), t, tn)
    w1 = each(_dot, a_ak, v_e)
    taw = each(lambda x, aa, w: _dot(x, side(embed(aa), embed(w))), t, a_c, w1)
    ry = each(lambda x, y: _dot(x, side(embed(lo(y)), embed(hi(y)))), a_rb, taw)
    rq = each(lambda rr, x: rr + lo(x), r_c, ry)
    y0 = each(lambda ark, vv, x: _dot(ark, vv) + hi(x), a_rk, v_e, ry)
    g = each(lambda x, bd: jnp.where(same_head, _dot(lo(x), bd, 0, 0), 0.0), taw, bd_c)
    s0 = each(lambda vv, x, kd, bd: jnp.where(same_head, _dot(jnp.concatenate([vv, hi(x)], axis=0),
                                                              jnp.concatenate([kd, bd], axis=0), 0, 0), 0.0),
              v_c, taw, kd_c, bd_c)

    states = [s_ref[b] for b in range(NB)]
    ys = [[] for _ in range(NB)]
    for ch in range(nc):
        for b in range(NB):
            i, s = b * nc + ch, states[b]
            ys[b].append(_dot(rq[i], s, 1, 1) + y0[i])
            states[b] = s * gamma_end[i][C - 1:C, :] + _dot(s, g[i]) + s0[i]
    for b in range(NB):
        s_ref[b] = states[b]
        y = jnp.concatenate(ys[b], axis=0)
        mu = headsum(y) * (1.0 / HEAD_DIM)
        yc = y - mu
        var = headsum(yc * yc) * (1.0 / HEAD_DIM)
        yn = yc * lax.rsqrt(var + RWKV_GN_EPS) * ln_w + ln_b
        o_ref[b] = ((yn + fronts[b][1]) * _silu(g_ref[b])).astype(o_ref.dtype)


def _rwkv(p3, par, mixwa, wup, aup, col0, gcol0, npairs, tt):
    B, T, _ = p3.shape
    col = lambda off: pl.BlockSpec((B, tt, LANES), lambda j, t: (0, t, off + j))
    fixed = lambda off: pl.BlockSpec((B, tt, LANES), lambda j, t: (0, t, off))
    return pl.pallas_call(
        _rwkv_body,
        grid=(npairs, T // tt),
        in_specs=[col(col0), col(col0 + npairs), col(col0 + 2 * npairs), fixed(col0 + 3 * npairs),
                  col(gcol0),
                  pl.BlockSpec((16, LANES), lambda j, t: (0, j)),
                  pl.BlockSpec((1, LANES), lambda j, t: (0, 0)),
                  pl.BlockSpec((LANES, LANES), lambda j, t: (0, j)),
                  pl.BlockSpec((LANES, LANES), lambda j, t: (0, j))],
        out_specs=pl.BlockSpec((B, tt, LANES), lambda j, t: (0, t, j)),
        out_shape=jax.ShapeDtypeStruct((B, T, npairs * LANES), BF16),
        scratch_shapes=[pltpu.VMEM((4 * B, tt + 8, LANES), F32), pltpu.VMEM((B, LANES, LANES), F32)],
        compiler_params=pltpu.CompilerParams(
            dimension_semantics=("parallel", "arbitrary"), vmem_limit_bytes=VMEM_LIMIT),
        name="rwkv7",
    )(p3, p3, p3, p3, p3, par, mixwa, wup, aup)


def _moba_body(q_ref, k_ref, v_ref, g_ref, bown_ref, bprev_ref, o_ref, kmean_ref, kb_ref, vt_ref, acc_ref,
               sc_ref):
    BS = MOBA_BLOCK
    NB, T = k_ref.shape[0], k_ref.shape[1]
    nb = T // BS
    i = pl.program_id(1)
    lane = lax.broadcasted_iota(jnp.int32, (1, LANES), 1)
    head0 = lane < HEAD_DIM
    streams = [(b, h) for b in range(NB) for h in range(2)]

    def each(f, *lists):
        return [f(*xs) for xs in zip(*lists)]

    @pl.when(i == 0)
    def _():
        for b in range(NB):
            kmean_ref[b] = jnp.mean(k_ref[b].reshape(nb, BS, LANES), axis=1)

        def fill(j, _):
            rows = pl.ds(pl.multiple_of(j * BS, BS), BS)
            for b in range(NB):
                kb_ref[b, rows, :] = k_ref[b, rows, :].astype(BF16)
                vt_ref[b, j] = v_ref[b, rows, :].T.astype(BF16)
            return 0

        lax.fori_loop(0, nb, fill, 0)

    qh = []
    for b in range(NB):
        q = q_ref[b]
        qh += [jnp.where(head0, q, 0.0), jnp.where(head0, 0.0, q)]
    qs = [(x * (HEAD_DIM ** -0.5 * LOG2_E)).astype(BF16) for x in qh]
    blk = lax.broadcasted_iota(jnp.int32, (nb, 1), 0).astype(F32)
    i_f = i.astype(F32)

    gate = [lax.dot_general(kmean_ref[b], x, (((1,), (1,)), ((), ())),
                            precision=lax.Precision.HIGHEST, preferred_element_type=F32)
            for (b, _), x in zip(streams, qh)]
    gate = each(lambda x: jnp.where(blk < i_f, x, -jnp.inf), gate)
    sel = []
    for rnk in range(MOBA_TOPK):
        top = each(lambda x: jnp.max(x, axis=0, keepdims=True), gate)
        idx = each(lambda x, tp: jnp.min(jnp.where(x == tp, blk, float(nb)), axis=0, keepdims=True), gate, top)
        gate = each(lambda x, ix: jnp.where(blk == ix, -jnp.inf, x), gate, idx)
        sel.append(each(lambda ix: jnp.where(rnk < i, ix, -5.0), idx))

    def chosen(s, j):
        j_f = j.astype(F32)
        return (sel[0][s] == j_f) | (sel[1][s] == j_f) | (sel[2][s] == j_f)

    def scores(j):
        rows = pl.ds(pl.multiple_of(j * BS, BS), BS)
        return [_dot(kb_ref[b, rows, :], qs[s], 1, 1) for s, (b, _) in enumerate(streams)]

    def weighted_values(j, p):
        return [_dot(vt_ref[b, j, h * HEAD_DIM:(h + 1) * HEAD_DIM, :], p[s]) for s, (b, h) in enumerate(streams)]

    def acc_rows(s):
        b, h = streams[s]
        return b, slice(h * HEAD_DIM, (h + 1) * HEAD_DIM)

    prev_blk = jnp.maximum(i - 1, 0)
    s_own = each(lambda x, bh: x + bown_ref[bh[1]], scores(i), streams)
    s_prev = each(lambda x, bh: x + bprev_ref[bh[1]], scores(prev_blk), streams)
    for s, x in enumerate(scores(0)):
        sc_ref[s] = x
    m = each(lambda x: jnp.max(x, axis=0, keepdims=True), s_own)
    p = each(lambda x, mm: jnp.exp2(x - mm), s_own, m)
    l = each(lambda x: jnp.sum(x, axis=0, keepdims=True), p)
    for s, pv in enumerate(weighted_values(i, p)):
        b, rows = acc_rows(s)
        acc_ref[b, rows, :] = pv

    def attend(j, sc, m, l, valid):
        take = [chosen(s, j) & valid for s in range(len(streams))]
        top = each(lambda x: jnp.max(x, axis=0, keepdims=True), sc)
        m_new = each(lambda mm, tk, tp: jnp.maximum(mm, jnp.where(tk, tp, -jnp.inf)), m, take, top)
        p = each(lambda x, tk, mn: jnp.exp2(x - jnp.where(tk, mn, jnp.inf)), sc, take, m_new)
        alpha = each(lambda mm, mn: jnp.exp2(mm - mn), m, m_new)
        l_new = each(lambda al, ll, x: al * ll + jnp.sum(x, axis=0, keepdims=True), alpha, l, p)
        for s, pv in enumerate(weighted_values(j, p)):
            b, rows = acc_rows(s)
            acc_ref[b, rows, :] = alpha[s] * acc_ref[b, rows, :] + pv
        return m_new, l_new

    m, l = attend(prev_blk, s_prev, m, l, i >= 1)

    def far_block(j, st):
        nxt = scores(jnp.minimum(j + 1, nb - 1))
        m_new, l_new = attend(j, [sc_ref[s] for s in range(len(streams))], list(st[0]), list(st[1]), True)
        for s, x in enumerate(nxt):
            sc_ref[s] = x
        return tuple(m_new), tuple(l_new)

    m, l = lax.fori_loop(0, jnp.maximum(i - 1, 0), far_block, (tuple(m), tuple(l)))
    for b in range(NB):
        l_rows = jnp.concatenate([jnp.broadcast_to(l[2 * b], (HEAD_DIM, BS)),
                                  jnp.broadcast_to(l[2 * b + 1], (HEAD_DIM, BS))], axis=0)
        out = (acc_ref[b] / l_rows).T
        o_ref[b] = (out * _silu(g_ref[b])).astype(o_ref.dtype)


def _moba(p3, bown, bprev, col0, gcol0, npairs):
    B, T, _ = p3.shape
    BS = MOBA_BLOCK
    nb = T // BS
    return pl.pallas_call(
        _moba_body,
        grid=(npairs, nb),
        in_specs=[pl.BlockSpec((B, BS, LANES), lambda j, i: (0, i, col0 + j)),
                  pl.BlockSpec((B, T, LANES), lambda j, i: (0, 0, col0 + npairs + j)),
                  pl.BlockSpec((B, T, LANES), lambda j, i: (0, 0, col0 + 2 * npairs + j)),
                  pl.BlockSpec((B, BS, LANES), lambda j, i: (0, i, gcol0 + j)),
                  pl.BlockSpec((2, BS, BS), lambda j, i: (j, 0, 0)),
                  pl.BlockSpec((2, BS, BS), lambda j, i: (j, 0, 0))],
        out_specs=pl.BlockSpec((B, BS, LANES), lambda j, i: (0, i, j)),
        out_shape=jax.ShapeDtypeStruct((B, T, npairs * LANES), BF16),
        scratch_shapes=[pltpu.VMEM((B, nb, LANES), F32), pltpu.VMEM((B, T, LANES), BF16),
                        pltpu.VMEM((B, nb, LANES, BS), BF16), pltpu.VMEM((B, LANES, BS), F32),
                        pltpu.VMEM((2 * B, BS, BS), F32)],
        compiler_params=pltpu.CompilerParams(
            dimension_semantics=("parallel", "arbitrary"), vmem_limit_bytes=VMEM_LIMIT),
        name="moba",
    )(p3, p3, p3, p3, bown, bprev)


def _rel_bucket_of(dist):
    max_exact = REL_BUCKETS // 2
    large = max_exact + (jnp.log(jnp.maximum(dist, 1).astype(F32) / max_exact)
                         / math.log(REL_MAX_DIST / max_exact) * (REL_BUCKETS - max_exact)).astype(jnp.int32)
    large = jnp.minimum(large, REL_BUCKETS - 1)
    return jnp.where(dist < max_exact, dist, large)


def _toeplitz_kq(tab):
    H, L = tab.shape
    BS = L // 2
    flat = jnp.tile(jnp.roll(tab, -1, axis=1), (1, BS))[:, :BS * (L - 1)]
    return flat.reshape(H, BS, L - 1)[:, :, BS - 1:2 * BS - 1]


def _moba_bias_tables(rel_bias):
    BS = MOBA_BLOCK
    per_dist = rel_bias.astype(F32)[_rel_bucket_of(jnp.arange(2 * BS))].T
    per_dist = (per_dist - per_dist[:, 2 * BS - 1:2 * BS]) * LOG2_E
    prev = _toeplitz_kq(per_dist)
    ki = np.arange(BS)[:, None]
    qi = np.arange(BS)[None, :]
    own = jnp.where(ki <= qi, _toeplitz_kq(jnp.roll(per_dist, BS, axis=1)), -jnp.inf)
    return own, prev


def _block_diag(w):
    g, n, _ = w.shape
    return jnp.einsum('gij,gh->gihj', w, jnp.eye(g, dtype=w.dtype)).reshape(g * n, g * n)


def kernel(x, norm_w, w_in, w_out, lru_conv_w, lru_conv_b, lru_gate_a_w, lru_gate_a_b, lru_gate_x_w, lru_gate_x_b, lru_lambda, rwkv_mix, rwkv_w0, rwkv_w_up, rwkv_a0, rwkv_a_up, rwkv_k_k, rwkv_k_a, rwkv_r_k, rwkv_ln_w, rwkv_ln_b, rel_bias, final_norm_w):
    B, T, D = x.shape
    depth = w_in.shape[0]
    lru_w = lru_conv_w.shape[2]
    rwkv_w = rwkv_w0.shape[1]
    moba_w = rel_bias.shape[1] * HEAD_DIM
    rwkv_pairs = rwkv_w // LANES
    moba_pairs = moba_w // LANES
    rwkv_col = (2 * lru_w) // LANES
    rwkv_gate_col = rwkv_col + (3 * rwkv_w + 2 * RWKV_LORA) // LANES
    moba_col = rwkv_gate_col + rwkv_w // LANES
    moba_gate_col = moba_col + (3 * moba_w) // LANES
    n_cols = w_in.shape[2]
    assert (moba_gate_col + moba_w // LANES) * LANES == n_cols
    assert T % MOBA_BLOCK == 0 and 2 * RWKV_LORA == LANES

    M = B * T
    tm_in = 1024
    tm_out = 512
    tn = n_cols // 3
    tt_lru = 512
    tt_rwkv = 512

    bown, bprev = _moba_bias_tables(rel_bias)
    w_in_b = w_in.astype(BF16)
    w_out_b = w_out.astype(BF16)
    zeros_lora = jnp.zeros((RWKV_LORA, rwkv_w), F32)

    x2 = x.reshape(M, D)
    h = _norm_cast(x2, norm_w[0].reshape(1, D), tm_out)
    for l in range(depth):
        p = _in_proj(h, w_in_b[l], tm_in, tn)
        p3 = p.reshape(B, T, n_cols)

        vec = jnp.stack([lru_conv_b[l], lru_gate_a_b[l], lru_gate_x_b[l], lru_lambda[l]]
                        + [jnp.zeros_like(lru_lambda[l])] * 4)
        ya = _lru(p3, lru_conv_w[l], vec, _block_diag(lru_gate_a_w[l]).astype(BF16),
                  _block_diag(lru_gate_x_w[l]).astype(BF16), lru_w, tt_lru)

        mix = rwkv_mix[l]
        par = jnp.stack([rwkv_w0[l], rwkv_a0[l], rwkv_k_k[l], rwkv_k_a[l], rwkv_ln_w[l], rwkv_ln_b[l],
                         rwkv_r_k[l].reshape(rwkv_w), mix[0:rwkv_w], mix[rwkv_w:2 * rwkv_w],
                         mix[2 * rwkv_w:3 * rwkv_w]] + [jnp.zeros((rwkv_w,), F32)] * 6)
        wup = jnp.concatenate([rwkv_w_up[l], zeros_lora], axis=0).astype(BF16)
        aup = jnp.concatenate([zeros_lora, rwkv_a_up[l]], axis=0).astype(BF16)
        yb = _rwkv(p3, par, mix[3 * rwkv_w:].reshape(1, LANES), wup, aup,
                   rwkv_col, rwkv_gate_col, rwkv_pairs, tt_rwkv)

        yc = _moba(p3, bown, bprev, moba_col, moba_gate_col, moba_pairs)

        final = l == depth - 1
        nw = (final_norm_w if final else norm_w[l + 1]).reshape(1, D)
        res = _out_proj(x2, ya.reshape(M, lru_w), yb.reshape(M, rwkv_w), yc.reshape(M, moba_w),
                        w_out_b[l], nw, tm_out, final)
        if final:
            return res.reshape(B, T, D)
        x2, h = res
```

```python
import functools
import math

import numpy as np
import jax
import jax.numpy as jnp
from jax import lax
from jax.experimental import pallas as pl
from jax.experimental.pallas import tpu as pltpu

F32 = jnp.float32
BF16 = jnp.bfloat16

LANES = 128
HEAD_DIM = 64
NORM_EPS = 1e-6
LRU_CONV = 4
LRU_C = 8.0
RWKV_LORA = 64
RWKV_DECAY_SCALE = math.exp(-0.5)
RWKV_GN_EPS = 64e-5
RWKV_CHUNK = 64
MOBA_BLOCK = 256
MOBA_TOPK = 3
MOBA_VROWS = 80
REL_BUCKETS = 32
REL_MAX_DIST = 128
LOG2_E = math.log2(math.e)
VMEM_LIMIT = 56 * 1024 * 1024


def _dot(a, b, ca=1, cb=0):
    return lax.dot_general(a.astype(BF16), b.astype(BF16), (((ca,), (cb,)), ((), ())),
                           preferred_element_type=F32)


def _sigmoid(x):
    return 0.5 * jnp.tanh(0.5 * x) + 0.5


def _silu(x):
    return x * _sigmoid(x)


def _rms_norm(x, w):
    ms = jnp.mean(x * x, axis=-1, keepdims=True)
    return x * lax.rsqrt(ms + NORM_EPS) * w


def _norm_body(x_ref, nw_ref, h_ref):
    h_ref[...] = _rms_norm(x_ref[...], nw_ref[...]).astype(BF16)


def _norm_cast(x2, nw, tm):
    M, D = x2.shape
    return pl.pallas_call(
        _norm_body,
        grid=(M // tm,),
        in_specs=[pl.BlockSpec((tm, D), lambda i: (i, 0)), pl.BlockSpec((1, D), lambda i: (0, 0))],
        out_specs=pl.BlockSpec((tm, D), lambda i: (i, 0)),
        out_shape=jax.ShapeDtypeStruct((M, D), BF16),
        compiler_params=pltpu.CompilerParams(dimension_semantics=("parallel",), vmem_limit_bytes=VMEM_LIMIT),
        name="norm_cast",
    )(x2, nw)


def _layer(l, *block):
    def spec(tail):
        return pl.BlockSpec((None,) + block, lambda *g: (l,) + tail(*g))
    return spec


def _inproj_body(h_ref, w_ref, o_ref):
    o_ref[...] = jnp.dot(h_ref[...], w_ref[...], preferred_element_type=F32)


def _in_proj(h, w_all, l, tm, tn):
    M, D = h.shape
    N = w_all.shape[2]
    return pl.pallas_call(
        _inproj_body,
        grid=(N // tn, M // tm),
        in_specs=[pl.BlockSpec((tm, D), lambda j, i: (i, 0)),
                  _layer(l, D, tn)(lambda j, i: (0, j))],
        out_specs=pl.BlockSpec((tm, tn), lambda j, i: (i, j)),
        out_shape=jax.ShapeDtypeStruct((M, N), F32),
        compiler_params=pltpu.CompilerParams(
            dimension_semantics=("parallel", "parallel"), vmem_limit_bytes=VMEM_LIMIT),
        name="in_proj",
    )(h, w_all)


def _outproj_body(x_ref, ya_ref, yb_ref, yc_ref, w_ref, nw_ref, *out_refs, final):
    wa, wb = ya_ref.shape[1], yb_ref.shape[1]
    acc = x_ref[...]
    acc = acc + _dot(ya_ref[...], w_ref[0:wa, :])
    acc = acc + _dot(yb_ref[...], w_ref[wa:wa + wb, :])
    acc = acc + _dot(yc_ref[...], w_ref[wa + wb:, :])
    normed = _rms_norm(acc, nw_ref[...])
    if final:
        out_refs[0][...] = normed
    else:
        out_refs[0][...] = acc
        out_refs[1][...] = normed.astype(BF16)


def _out_proj(x2, ya, yb, yc, w_all, l, nw, tm, final):
    M, D = x2.shape
    row = lambda width: pl.BlockSpec((tm, width), lambda i: (i, 0))
    full = lambda a: pl.BlockSpec(a.shape, lambda i: (0, 0))
    if final:
        out_specs, out_shape = row(D), jax.ShapeDtypeStruct((M, D), F32)
    else:
        out_specs = [row(D), row(D)]
        out_shape = [jax.ShapeDtypeStruct((M, D), F32), jax.ShapeDtypeStruct((M, D), BF16)]
    return pl.pallas_call(
        functools.partial(_outproj_body, final=final),
        grid=(M // tm,),
        in_specs=[row(D), row(ya.shape[1]), row(yb.shape[1]), row(yc.shape[1]),
                  _layer(l, *w_all.shape[1:])(lambda i: (0, 0)), full(nw)],
        out_specs=out_specs,
        out_shape=out_shape,
        compiler_params=pltpu.CompilerParams(
            dimension_semantics=("parallel",), vmem_limit_bytes=VMEM_LIMIT),
        name="out_proj_final" if final else "out_proj",
    )(x2, ya, yb, yc, w_all, nw)


def _lru_body(xa_ref, ga_ref, cw_ref, vec_ref, wa_ref, wx_ref, o_ref, xbuf, hcar):
    tt = xa_ref.shape[1]
    W = xa_ref.shape[2]

    @pl.when(pl.program_id(1) == 0)
    def _():
        xbuf[0:8, :] = jnp.zeros((8, W), F32)
        hcar[...] = jnp.zeros((8, W), F32)

    xa = xa_ref[0]
    xbuf[8:8 + tt, :] = xa
    xc = vec_ref[0:1, :] + cw_ref[LRU_CONV - 1:LRU_CONV, :] * xa
    for s in range(1, LRU_CONV):
        xc = xc + cw_ref[LRU_CONV - 1 - s:LRU_CONV - s, :] * xbuf[8 - s:8 - s + tt, :]
    xbuf[0:8, :] = xa[tt - 8:tt, :]

    r = _sigmoid(_dot(xc, wa_ref[...]) + vec_ref[1:2, :])
    gi = _sigmoid(_dot(xc, wx_ref[...]) + vec_ref[2:3, :])
    lam = vec_ref[3:4, :]
    softplus_neg_lam = jnp.maximum(-lam, 0.0) + jnp.log1p(jnp.exp(-jnp.abs(lam)))
    log_a = (-LRU_C) * r * softplus_neg_lam
    a = jnp.exp(log_a)
    b = jnp.sqrt(-jnp.tanh(log_a) * (a * a + 1.0)) * (gi * xc)

    row = lax.broadcasted_iota(jnp.int32, (tt, 1), 0)
    d = 1
    while d < tt:
        keep = row >= d
        a_sh = pltpu.roll(a, d, 0)
        b_sh = pltpu.roll(b, d, 0)
        b = b + jnp.where(keep, a * b_sh, 0.0)
        a = jnp.where(keep, a * a_sh, a)
        d *= 2
    h = b + a * hcar[0:1, :]
    hcar[...] = jnp.broadcast_to(h[tt - 1:tt, :], (8, W))
    o_ref[0] = (h * _silu(ga_ref[0])).astype(o_ref.dtype)


def _lru(p3, cw, vec, wa, wx, l, width, tt):
    B, T, _ = p3.shape
    full = lambda a: _layer(l, *a.shape[1:])(lambda b, t: (0, 0))
    return pl.pallas_call(
        _lru_body,
        grid=(B, T // tt),
        in_specs=[pl.BlockSpec((1, tt, width), lambda b, t: (b, t, 0)),
                  pl.BlockSpec((1, tt, width), lambda b, t: (b, t, 1)),
                  full(cw), full(vec), full(wa), full(wx)],
        out_specs=pl.BlockSpec((1, tt, width), lambda b, t: (b, t, 0)),
        out_shape=jax.ShapeDtypeStruct((B, T, width), BF16),
        scratch_shapes=[pltpu.VMEM((tt + 8, width), F32), pltpu.VMEM((8, width), F32)],
        compiler_params=pltpu.CompilerParams(
            dimension_semantics=("parallel", "arbitrary"), vmem_limit_bytes=VMEM_LIMIT),
        name="rg_lru",
    )(p3, p3, cw, vec, wa, wx)


def _rwkv_body(r_ref, k_ref, v_ref, wa_ref, g_ref, par_ref, mixwa_ref, wup_ref, aup_ref, o_ref,
               xs_ref, s_ref):
    C = RWKV_CHUNK
    NB, tt = r_ref.shape[0], r_ref.shape[1]
    nc = tt // C

    @pl.when(pl.program_id(1) == 0)
    def _():
        xs_ref[:, 0:8, :] = jnp.zeros((xs_ref.shape[0], 8, LANES), F32)
        s_ref[...] = jnp.zeros(s_ref.shape, F32)

    lane = lax.broadcasted_iota(jnp.int32, (1, LANES), 1)
    head0 = lane < HEAD_DIM
    rc = lax.broadcasted_iota(jnp.int32, (tt, 1), 0) & (C - 1)

    par = par_ref[...]
    w0, a0, k_k, k_a = par[0:1], par[1:2], par[2:3], par[3:4]
    ln_w, ln_b, r_k = par[4:5], par[5:6], par[6:7]
    mix_r, mix_k, mix_v = par[7:8], par[8:9], par[9:10]

    def shift_lerp(ref, b, mix, slot):
        s = ref[b]
        xs_ref[4 * b + slot, 8:8 + tt, :] = s
        prev = xs_ref[4 * b + slot, 7:7 + tt, :]
        xs_ref[4 * b + slot, 0:8, :] = s[tt - 8:tt, :]
        return s + mix * (prev - s)

    def headsum(x):
        s0 = jnp.sum(jnp.where(head0, x, 0.0), axis=-1, keepdims=True)
        s1 = jnp.sum(jnp.where(head0, 0.0, x), axis=-1, keepdims=True)
        return jnp.where(head0, s0, s1)

    def chunks(x):
        return [x[ch * C:(ch + 1) * C] for ch in range(nc)]

    def front(b):
        r = shift_lerp(r_ref, b, mix_r, 0)
        k = shift_lerp(k_ref, b, mix_k, 1)
        v = shift_lerp(v_ref, b, mix_v, 2)
        wa = shift_lerp(wa_ref, b, mixwa_ref[...], 3)
        log_w = (-RWKV_DECAY_SCALE) * _sigmoid(w0 + _dot(jnp.tanh(wa), wup_ref[...]))
        a = _sigmoid(a0 + _dot(wa, aup_ref[...]))
        kk = k * k_k
        kk = kk * lax.rsqrt(jnp.maximum(headsum(kk * kk), 1e-24))
        k = k * (1.0 + (a - 1.0) * k_a)
        be = kk * a
        c = log_w
        d = 1
        while d < C:
            c = c + jnp.where(rc >= d, pltpu.roll(c, d, 0), 0.0)
            d *= 2
        c_last = jnp.broadcast_to(c.reshape(nc, C, LANES)[:, C - 1:C, :], (nc, C, LANES)).reshape(tt, LANES)
        e_inc = jnp.exp(c)
        e_inv = jnp.exp(-c)
        e_end = jnp.exp(c_last - c)
        parts = dict(a=-kk * jnp.exp(c - log_w), r=r * e_inc, kt=k * e_inv, bt=be * e_inv,
                     kd=k * e_end, bd=be * e_end, v=v, gamma_end=jnp.exp(c_last))
        return {name: chunks(val) for name, val in parts.items()}, headsum(r * k * r_k) * v

    fronts = [front(b) for b in range(NB)]
    gather = lambda name: [x for f, _ in fronts for x in f[name]]

    ti = lax.broadcasted_iota(jnp.int32, (C, LANES), 0)
    si = lax.broadcasted_iota(jnp.int32, (C, LANES), 1) & (C - 1)
    strict = ti > si
    incl = ti >= si
    same16 = (ti >> 4) == (si >> 4)
    same32 = (ti >> 5) == (si >> 5)
    eye = (ti == si).astype(F32)
    same_head = ((lax.broadcasted_iota(jnp.int32, (LANES, LANES), 0) >> 6)
                 == (lax.broadcasted_iota(jnp.int32, (LANES, LANES), 1) >> 6))
    head0_b = jnp.broadcast_to(head0, (C, LANES))

    def embed(x):
        xb = x.astype(BF16)
        zero = jnp.zeros_like(xb)
        return jnp.concatenate([jnp.where(head0_b, xb, zero), jnp.where(head0_b, zero, xb)], axis=0)

    def each(f, *lists):
        return [f(*xs) for xs in zip(*lists)]

    lo = lambda x: x[:, 0:LANES]
    hi = lambda x: x[:, LANES:2 * LANES]
    side = lambda x, y: jnp.concatenate([x, y], axis=1)

    a_c = gather('a')
    r_c = gather('r')
    k_e = each(embed, gather('kt'))
    b_e = each(embed, gather('bt'))
    kd_c = gather('kd')
    bd_c = gather('bd')
    v_c = gather('v')
    v_e = each(embed, v_c)
    gamma_end = gather('gamma_end')

    p = each(lambda aa, rr, kt, bt: _dot(jnp.concatenate([aa, rr], axis=0),
                                         jnp.concatenate([kt, bt], axis=0), 1, 1), a_c, r_c, k_e, b_e)
    a_ak = each(lambda x: jnp.where(strict, x[0:C, 0:LANES], 0.0), p)
    n = each(lambda x: jnp.where(strict, x[0:C, LANES:2 * LANES], 0.0), p)
    a_rk = each(lambda x: jnp.where(incl, x[C:2 * C, 0:LANES], 0.0), p)
    a_rb = each(lambda x: jnp.where(incl, x[C:2 * C, LANES:2 * LANES], 0.0), p)
    q = each(lambda x: jnp.where(same16, x, 0.0), n)
    t = each(lambda x: eye + x, q)
    q = each(lambda x: _dot(x, embed(x)), q)
    for _ in range(2):
        qt = each(lambda x, y: _dot(x, side(embed(y), embed(x))), q, t)
        t = each(lambda x, y: x + lo(y), t, qt)
        q = each(hi, qt)
    t = each(lambda x, y: x + _dot(y, embed(x)), t, q)
    tn = each(lambda x, y: _dot(x, embed(jnp.where(same32 & jnp.logical_not(same16), y, 0.0))), t, n)
    t = each(lambda x, y: x + _dot(y, embed(x)), t, tn)
    tn = each(lambda x, y: _dot(x, embed(jnp.where(same32, 0.0, y))), t, n)
    t = each(lambda x, y: x + _dot(y, embed(x)), t, tn)
    w1 = each(_dot, a_ak, v_e)
    taw = each(lambda x, aa, w: _dot(x, side(embed(aa), embed(w))), t, a_c, w1)
    ry = each(lambda x, y: _dot(x, side(embed(lo(y)), embed(hi(y)))), a_rb, taw)
    rq = each(lambda rr, x: rr + lo(x), r_c, ry)
    y0 = each(lambda ark, vv, x: _dot(ark, vv) + hi(x), a_rk, v_e, ry)
    g = each(lambda x, bd: jnp.where(same_head, _dot(lo(x), bd, 0, 0), 0.0), taw, bd_c)
    s0 = each(lambda vv, x, kd, bd: jnp.where(same_head, _dot(jnp.concatenate([vv, hi(x)], axis=0),
                                                              jnp.concatenate([kd, bd], axis=0), 0, 0), 0.0),
              v_c, taw, kd_c, bd_c)

    states = [s_ref[b] for b in range(NB)]
    ys = [[] for _ in range(NB)]
    for ch in range(nc):
        for b in range(NB):
            i, s = b * nc + ch, states[b]
            ys[b].append(_dot(rq[i], s, 1, 1) + y0[i])
            states[b] = s * gamma_end[i][C - 1:C, :] + _dot(s, g[i]) + s0[i]
    for b in range(NB):
        s_ref[b] = states[b]
        y = jnp.concatenate(ys[b], axis=0)
        mu = headsum(y) * (1.0 / HEAD_DIM)
        yc = y - mu
        var = headsum(yc * yc) * (1.0 / HEAD_DIM)
        yn = yc * lax.rsqrt(var + RWKV_GN_EPS) * ln_w + ln_b
        o_ref[b] = ((yn + fronts[b][1]) * _silu(g_ref[b])).astype(o_ref.dtype)


def _rwkv(p3, par, mixwa, wup, aup, l, col0, gcol0, npairs, tt):
    B, T, _ = p3.shape
    col = lambda off: pl.BlockSpec((B, tt, LANES), lambda j, t: (0, t, off + j))
    fixed = lambda off: pl.BlockSpec((B, tt, LANES), lambda j, t: (0, t, off))
    return pl.pallas_call(
        _rwkv_body,
        grid=(npairs, T // tt),
        in_specs=[col(col0), col(col0 + npairs), col(col0 + 2 * npairs), fixed(col0 + 3 * npairs),
                  col(gcol0),
                  _layer(l, 16, LANES)(lambda j, t: (0, j)),
                  _layer(l, 1, LANES)(lambda j, t: (0, 0)),
                  _layer(l, LANES, LANES)(lambda j, t: (0, j)),
                  _layer(l, LANES, LANES)(lambda j, t: (0, j))],
        out_specs=pl.BlockSpec((B, tt, LANES), lambda j, t: (0, t, j)),
        out_shape=jax.ShapeDtypeStruct((B, T, npairs * LANES), BF16),
        scratch_shapes=[pltpu.VMEM((4 * B, tt + 8, LANES), F32), pltpu.VMEM((B, LANES, LANES), F32)],
        compiler_params=pltpu.CompilerParams(
            dimension_semantics=("parallel", "arbitrary"), vmem_limit_bytes=VMEM_LIMIT),
        name="rwkv7",
    )(p3, p3, p3, p3, p3, par, mixwa, wup, aup)


def _moba_body(q_ref, k_ref, v_ref, g_ref, bown_ref, bprev_ref, o_ref, kmean_ref, kb_ref, vt_ref, acc_ref,
               sc_ref, p_ref):
    BS = MOBA_BLOCK
    NB, T = k_ref.shape[0], k_ref.shape[1]
    nb = T // BS
    i = pl.program_id(1)
    lane = lax.broadcasted_iota(jnp.int32, (1, LANES), 1)
    head0 = lane < HEAD_DIM
    streams = [(b, h) for b in range(NB) for h in range(2)]

    def each(f, *lists):
        return [f(*xs) for xs in zip(*lists)]

    VR = MOBA_VROWS

    @pl.when(i == 0)
    def _():
        for b in range(NB):
            kmean_ref[b] = jnp.mean(k_ref[b].reshape(nb, BS, LANES), axis=1)

        def fill(j, _):
            rows = pl.ds(pl.multiple_of(j * BS, BS), BS)
            ones = jnp.ones((VR - HEAD_DIM, BS), BF16)
            for b in range(NB):
                kb_ref[b, rows, :] = k_ref[b, rows, :].astype(BF16)
                vt = v_ref[b, rows, :].T.astype(BF16)
                vt_ref[b, j] = jnp.concatenate([vt[0:HEAD_DIM], ones, vt[HEAD_DIM:2 * HEAD_DIM], ones], axis=0)
            return 0

        lax.fori_loop(0, nb, fill, 0)

    qh = []
    for b in range(NB):
        q = q_ref[b]
        qh += [jnp.where(head0, q, 0.0), jnp.where(head0, 0.0, q)]
    qs = [(x * (HEAD_DIM ** -0.5 * LOG2_E)).astype(BF16) for x in qh]

    def scores(j):
        rows = pl.ds(pl.multiple_of(j * BS, BS), BS)
        return [_dot(kb_ref[b, rows, :], qs[s], 1, 1) for s, (b, _) in enumerate(streams)]

    def weighted(j, s, p):
        b, h = streams[s]
        return _dot(vt_ref[b, j, h * VR:(h + 1) * VR, :], p)

    def acc_at(s):
        b, h = streams[s]
        return b, slice(h * VR, (h + 1) * VR)

    def split(x):
        hi = x.astype(BF16)
        return hi, (x - hi.astype(F32)).astype(BF16)

    def gate_of(b, x):
        k_hi, k_lo = split(kmean_ref[b])
        q_hi, q_lo = split(x)
        return _dot(k_hi, q_hi, 1, 1) + (_dot(k_hi, q_lo, 1, 1) + _dot(k_lo, q_hi, 1, 1))

    prev_blk = jnp.maximum(i - 1, 0)
    s_own = each(lambda x, bh: x + bown_ref[bh[1]], scores(i), streams)
    gate = [gate_of(b, x) for (b, _), x in zip(streams, qh)]
    s_prev = each(lambda x, bh: x + bprev_ref[bh[1]], scores(prev_blk), streams)
    for s, x in enumerate(scores(0)):
        sc_ref[s] = x

    m = each(lambda x: jnp.max(x, axis=0, keepdims=True), s_own)
    for s in range(len(streams)):
        b, rows = acc_at(s)
        acc_ref[b, rows, :] = weighted(i, s, jnp.exp2(s_own[s] - m[s]))

    blk = lax.broadcasted_iota(jnp.int32, (nb, 1), 0).astype(F32)
    i_f = i.astype(F32)
    gate = each(lambda x: jnp.where(blk < i_f, x, -jnp.inf), gate)
    sel = []
    for rnk in range(MOBA_TOPK):
        top = each(lambda x: jnp.max(x, axis=0, keepdims=True), gate)
        idx = each(lambda x, tp: jnp.min(jnp.where(x == tp, blk, float(nb)), axis=0, keepdims=True), gate, top)
        gate = each(lambda x, ix: jnp.where(blk == ix, -jnp.inf, x), gate, idx)
        sel.append(each(lambda ix: jnp.where(rnk < i, ix, -5.0), idx))

    def chosen(s, j):
        j_f = j.astype(F32)
        return (sel[0][s] == j_f) | (sel[1][s] == j_f) | (sel[2][s] == j_f)

    def softmax_step(j, sc, m, valid):
        m_new, alpha = [], []
        for s in range(len(streams)):
            take = chosen(s, j) & valid
            top = jnp.max(sc[s], axis=0, keepdims=True)
            mn = jnp.maximum(m[s], jnp.where(take, top, -jnp.inf))
            p_ref[s] = jnp.exp2(sc[s] - jnp.where(take, mn, jnp.inf)).astype(BF16)
            m_new.append(mn)
            alpha.append(jnp.exp2(m[s] - mn))
        return m_new, alpha

    def accumulate(j, alpha):
        for s in range(len(streams)):
            b, rows = acc_at(s)
            acc_ref[b, rows, :] = alpha[s] * acc_ref[b, rows, :] + weighted(j, s, p_ref[s])

    m, alpha = softmax_step(prev_blk, s_prev, m, i >= 1)

    def far_block(j, st):
        m, alpha = list(st[0]), list(st[1])
        nxt = scores(jnp.minimum(j + 1, nb - 1))
        accumulate(jnp.where(j == 0, prev_blk, j - 1), alpha)
        m, alpha = softmax_step(j, [sc_ref[s] for s in range(len(streams))], m, True)
        for s, x in enumerate(nxt):
            sc_ref[s] = x
        return tuple(m), tuple(alpha)

    n_far = jnp.maximum(i - 1, 0)
    m, alpha = lax.fori_loop(0, n_far, far_block, (tuple(m), tuple(alpha)))
    accumulate(jnp.where(n_far == 0, prev_blk, n_far - 1), alpha)

    for b in range(NB):
        acc = acc_ref[b]
        out_t = jnp.concatenate([acc[h * VR:h * VR + HEAD_DIM] * (1.0 / acc[h * VR + HEAD_DIM:h * VR + HEAD_DIM + 1])
                                 for h in range(2)], axis=0)
        o_ref[b] = (out_t.T * _silu(g_ref[b])).astype(o_ref.dtype)


def _moba(p3, bown, bprev, col0, gcol0, npairs):
    B, T, _ = p3.shape
    BS = MOBA_BLOCK
    nb = T // BS
    return pl.pallas_call(
        _moba_body,
        grid=(npairs, nb),
        in_specs=[pl.BlockSpec((B, BS, LANES), lambda j, i: (0, i, col0 + j)),
                  pl.BlockSpec((B, T, LANES), lambda j, i: (0, 0, col0 + npairs + j)),
                  pl.BlockSpec((B, T, LANES), lambda j, i: (0, 0, col0 + 2 * npairs + j)),
                  pl.BlockSpec((B, BS, LANES), lambda j, i: (0, i, gcol0 + j)),
                  pl.BlockSpec((2, BS, BS), lambda j, i: (j, 0, 0)),
                  pl.BlockSpec((2, BS, BS), lambda j, i: (j, 0, 0))],
        out_specs=pl.BlockSpec((B, BS, LANES), lambda j, i: (0, i, j)),
        out_shape=jax.ShapeDtypeStruct((B, T, npairs * LANES), BF16),
        scratch_shapes=[pltpu.VMEM((B, nb, LANES), F32), pltpu.VMEM((B, T, LANES), BF16),
                        pltpu.VMEM((B, nb, 2 * MOBA_VROWS, BS), BF16), pltpu.VMEM((B, 2 * MOBA_VROWS, BS), F32),
                        pltpu.VMEM((2 * B, BS, BS), F32), pltpu.VMEM((2 * B, BS, BS), BF16)],
        compiler_params=pltpu.CompilerParams(
            dimension_semantics=("parallel", "arbitrary"), vmem_limit_bytes=VMEM_LIMIT),
        name="moba",
    )(p3, p3, p3, p3, bown, bprev)


def _rel_bucket_of(dist):
    max_exact = REL_BUCKETS // 2
    large = max_exact + (jnp.log(jnp.maximum(dist, 1).astype(F32) / max_exact)
                         / math.log(REL_MAX_DIST / max_exact) * (REL_BUCKETS - max_exact)).astype(jnp.int32)
    large = jnp.minimum(large, REL_BUCKETS - 1)
    return jnp.where(dist < max_exact, dist, large)


def _toeplitz_kq(tab):
    H, L = tab.shape
    BS = L // 2
    flat = jnp.tile(jnp.roll(tab, -1, axis=1), (1, BS))[:, :BS * (L - 1)]
    return flat.reshape(H, BS, L - 1)[:, :, BS - 1:2 * BS - 1]


def _moba_bias_tables(rel_bias):
    BS = MOBA_BLOCK
    per_dist = rel_bias.astype(F32)[_rel_bucket_of(jnp.arange(2 * BS))].T
    per_dist = (per_dist - per_dist[:, 2 * BS - 1:2 * BS]) * LOG2_E
    prev = _toeplitz_kq(per_dist)
    ki = np.arange(BS)[:, None]
    qi = np.arange(BS)[None, :]
    own = jnp.where(ki <= qi, _toeplitz_kq(jnp.roll(per_dist, BS, axis=1)), -jnp.inf)
    return own, prev


def _block_diag(w):
    d, g, n, _ = w.shape
    return jnp.einsum('lgij,gh->lgihj', w, jnp.eye(g, dtype=w.dtype)).reshape(d, g * n, g * n)


def kernel(x, norm_w, w_in, w_out, lru_conv_w, lru_conv_b, lru_gate_a_w, lru_gate_a_b, lru_gate_x_w, lru_gate_x_b, lru_lambda, rwkv_mix, rwkv_w0, rwkv_w_up, rwkv_a0, rwkv_a_up, rwkv_k_k, rwkv_k_a, rwkv_r_k, rwkv_ln_w, rwkv_ln_b, rel_bias, final_norm_w):
    B, T, D = x.shape
    depth = w_in.shape[0]
    lru_w = lru_conv_w.shape[2]
    rwkv_w = rwkv_w0.shape[1]
    moba_w = rel_bias.shape[1] * HEAD_DIM
    rwkv_pairs = rwkv_w // LANES
    moba_pairs = moba_w // LANES
    rwkv_col = (2 * lru_w) // LANES
    rwkv_gate_col = rwkv_col + (3 * rwkv_w + 2 * RWKV_LORA) // LANES
    moba_col = rwkv_gate_col + rwkv_w // LANES
    moba_gate_col = moba_col + (3 * moba_w) // LANES
    n_cols = w_in.shape[2]
    assert (moba_gate_col + moba_w // LANES) * LANES == n_cols
    assert T % MOBA_BLOCK == 0 and 2 * RWKV_LORA == LANES

    M = B * T
    tm_in = 1024
    tn_in = n_cols // 3
    tm_out = 512
    tt_lru = 512
    tt_rwkv = 512

    bown, bprev = _moba_bias_tables(rel_bias)
    w_in_b = w_in.astype(BF16)
    w_out_b = w_out.astype(BF16)
    lru_vec = jnp.stack([lru_conv_b, lru_gate_a_b, lru_gate_x_b, lru_lambda] + [jnp.zeros_like(lru_lambda)] * 4,
                        axis=1)
    lru_wa = _block_diag(lru_gate_a_w).astype(BF16)
    lru_wx = _block_diag(lru_gate_x_w).astype(BF16)
    mix = rwkv_mix
    rwkv_par = jnp.stack([rwkv_w0, rwkv_a0, rwkv_k_k, rwkv_k_a, rwkv_ln_w, rwkv_ln_b,
                          rwkv_r_k.reshape(depth, rwkv_w), mix[:, 0:rwkv_w], mix[:, rwkv_w:2 * rwkv_w],
                          mix[:, 2 * rwkv_w:3 * rwkv_w]] + [jnp.zeros_like(rwkv_w0)] * 6, axis=1)
    mixwa = mix[:, 3 * rwkv_w:].reshape(depth, 1, LANES)
    zeros_lora = jnp.zeros_like(rwkv_w_up)
    wup = jnp.concatenate([rwkv_w_up, zeros_lora], axis=1).astype(BF16)
    aup = jnp.concatenate([zeros_lora, rwkv_a_up], axis=1).astype(BF16)

    x2 = x.reshape(M, D)
    h = _norm_cast(x2, norm_w[0].reshape(1, D), tm_out)
    for l in range(depth):
        p3 = _in_proj(h, w_in_b, l, tm_in, tn_in).reshape(B, T, n_cols)
        ya = _lru(p3, lru_conv_w, lru_vec, lru_wa, lru_wx, l, lru_w, tt_lru)
        yb = _rwkv(p3, rwkv_par, mixwa, wup, aup, l, rwkv_col, rwkv_gate_col, rwkv_pairs, tt_rwkv)
        yc = _moba(p3, bown, bprev, moba_col, moba_gate_col, moba_pairs)
        final = l == depth - 1
        nw = (final_norm_w if final else norm_w[l + 1]).reshape(1, D)
        res = _out_proj(x2, ya.reshape(M, lru_w), yb.reshape(M, rwkv_w), yc.reshape(M, moba_w),
                        w_out_b, l, nw, tm_out, final)
        if final:
            return res.reshape(B, T, D)
        x2, h = res
```

```python
import functools
import math

import numpy as np
import jax
import jax.numpy as jnp
from jax import lax
from jax.experimental import pallas as pl
from jax.experimental.pallas import tpu as pltpu

F32 = jnp.float32
BF16 = jnp.bfloat16

LANES = 128
HEAD_DIM = 64
NORM_EPS = 1e-6
LRU_CONV = 4
LRU_C = 8.0
RWKV_LORA = 64
RWKV_DECAY_SCALE = math.exp(-0.5)
RWKV_GN_EPS = 64e-5
RWKV_CHUNK = 64
MOBA_BLOCK = 256
MOBA_TOPK = 3
MOBA_VROWS = 80
REL_BUCKETS = 32
REL_MAX_DIST = 128
LOG2_E = math.log2(math.e)
VMEM_LIMIT = 56 * 1024 * 1024


def _dot(a, b, ca=1, cb=0):
    return lax.dot_general(a.astype(BF16), b.astype(BF16), (((ca,), (cb,)), ((), ())),
                           preferred_element_type=F32)


def _sigmoid(x):
    return 0.5 * jnp.tanh(0.5 * x) + 0.5


def _silu(x):
    return x * _sigmoid(x)


def _rms_norm(x, w):
    ms = jnp.mean(x * x, axis=-1, keepdims=True)
    return x * lax.rsqrt(ms + NORM_EPS) * w


def _norm_body(x_ref, nw_ref, h_ref):
    h_ref[...] = _rms_norm(x_ref[...], nw_ref[...]).astype(BF16)


def _norm_cast(x2, nw, tm):
    M, D = x2.shape
    return pl.pallas_call(
        _norm_body,
        grid=(M // tm,),
        in_specs=[pl.BlockSpec((tm, D), lambda i: (i, 0)), pl.BlockSpec((1, D), lambda i: (0, 0))],
        out_specs=pl.BlockSpec((tm, D), lambda i: (i, 0)),
        out_shape=jax.ShapeDtypeStruct((M, D), BF16),
        compiler_params=pltpu.CompilerParams(dimension_semantics=("parallel",), vmem_limit_bytes=VMEM_LIMIT),
        name="norm_cast",
    )(x2, nw)


def _layer(l, *block):
    def spec(tail):
        return pl.BlockSpec((None,) + block, lambda *g: (l,) + tail(*g))
    return spec


def _inproj_body(h_ref, w_ref, o_ref):
    o_ref[...] = jnp.dot(h_ref[...], w_ref[...], preferred_element_type=F32)


def _in_proj(h, w_all, l, tm, tn):
    M, D = h.shape
    N = w_all.shape[2]
    return pl.pallas_call(
        _inproj_body,
        grid=(N // tn, M // tm),
        in_specs=[pl.BlockSpec((tm, D), lambda j, i: (i, 0)),
                  _layer(l, D, tn)(lambda j, i: (0, j))],
        out_specs=pl.BlockSpec((tm, tn), lambda j, i: (i, j)),
        out_shape=jax.ShapeDtypeStruct((M, N), F32),
        compiler_params=pltpu.CompilerParams(
            dimension_semantics=("parallel", "parallel"), vmem_limit_bytes=VMEM_LIMIT),
        name="in_proj",
    )(h, w_all)


def _outproj_body(x_ref, ya_ref, yb_ref, yc_ref, w_ref, nw_ref, *out_refs, final):
    wa, wb = ya_ref.shape[1], yb_ref.shape[1]
    acc = x_ref[...]
    acc = acc + _dot(ya_ref[...], w_ref[0:wa, :])
    acc = acc + _dot(yb_ref[...], w_ref[wa:wa + wb, :])
    acc = acc + _dot(yc_ref[...], w_ref[wa + wb:, :])
    normed = _rms_norm(acc, nw_ref[...])
    if final:
        out_refs[0][...] = normed
    else:
        out_refs[0][...] = acc
        out_refs[1][...] = normed.astype(BF16)


def _out_proj(x2, ya, yb, yc, w_all, l, nw, tm, final):
    M, D = x2.shape
    row = lambda width: pl.BlockSpec((tm, width), lambda i: (i, 0))
    full = lambda a: pl.BlockSpec(a.shape, lambda i: (0, 0))
    if final:
        out_specs, out_shape = row(D), jax.ShapeDtypeStruct((M, D), F32)
    else:
        out_specs = [row(D), row(D)]
        out_shape = [jax.ShapeDtypeStruct((M, D), F32), jax.ShapeDtypeStruct((M, D), BF16)]
    return pl.pallas_call(
        functools.partial(_outproj_body, final=final),
        grid=(M // tm,),
        in_specs=[row(D), row(ya.shape[1]), row(yb.shape[1]), row(yc.shape[1]),
                  _layer(l, *w_all.shape[1:])(lambda i: (0, 0)), full(nw)],
        out_specs=out_specs,
        out_shape=out_shape,
        compiler_params=pltpu.CompilerParams(
            dimension_semantics=("parallel",), vmem_limit_bytes=VMEM_LIMIT),
        name="out_proj_final" if final else "out_proj",
    )(x2, ya, yb, yc, w_all, nw)


def _lru_body(xa_ref, ga_ref, cw_ref, vec_ref, wa_ref, wx_ref, o_ref, xbuf, hcar):
    tt = xa_ref.shape[1]
    W = xa_ref.shape[2]

    @pl.when(pl.program_id(1) == 0)
    def _():
        xbuf[0:8, :] = jnp.zeros((8, W), F32)
        hcar[...] = jnp.zeros((8, W), F32)

    xa = xa_ref[0]
    xbuf[8:8 + tt, :] = xa
    xc = vec_ref[0:1, :] + cw_ref[LRU_CONV - 1:LRU_CONV, :] * xa
    for s in range(1, LRU_CONV):
        xc = xc + cw_ref[LRU_CONV - 1 - s:LRU_CONV - s, :] * xbuf[8 - s:8 - s + tt, :]
    xbuf[0:8, :] = xa[tt - 8:tt, :]

    r = _sigmoid(_dot(xc, wa_ref[...]) + vec_ref[1:2, :])
    gi = _sigmoid(_dot(xc, wx_ref[...]) + vec_ref[2:3, :])
    lam = vec_ref[3:4, :]
    softplus_neg_lam = jnp.maximum(-lam, 0.0) + jnp.log1p(jnp.exp(-jnp.abs(lam)))
    log_a = (-LRU_C) * r * softplus_neg_lam
    a = jnp.exp(log_a)
    one_minus_a2 = -jnp.tanh(log_a) * (a * a + 1.0)
    b = jnp.where(one_minus_a2 > 0.0, one_minus_a2 * lax.rsqrt(one_minus_a2), 0.0) * (gi * xc)

    row = lax.broadcasted_iota(jnp.int32, (tt, 1), 0)
    d = 1
    while d < tt:
        keep = row >= d
        a_sh = pltpu.roll(a, d, 0)
        b_sh = pltpu.roll(b, d, 0)
        b = b + jnp.where(keep, a * b_sh, 0.0)
        a = jnp.where(keep, a * a_sh, a)
        d *= 2
    h = b + a * hcar[0:1, :]
    hcar[...] = jnp.broadcast_to(h[tt - 1:tt, :], (8, W))
    o_ref[0] = (h * _silu(ga_ref[0])).astype(o_ref.dtype)


def _lru(p3, cw, vec, wa, wx, l, width, tt):
    B, T, _ = p3.shape
    full = lambda a: _layer(l, *a.shape[1:])(lambda b, t: (0, 0))
    return pl.pallas_call(
        _lru_body,
        grid=(B, T // tt),
        in_specs=[pl.BlockSpec((1, tt, width), lambda b, t: (b, t, 0)),
                  pl.BlockSpec((1, tt, width), lambda b, t: (b, t, 1)),
                  full(cw), full(vec), full(wa), full(wx)],
        out_specs=pl.BlockSpec((1, tt, width), lambda b, t: (b, t, 0)),
        out_shape=jax.ShapeDtypeStruct((B, T, width), BF16),
        scratch_shapes=[pltpu.VMEM((tt + 8, width), F32), pltpu.VMEM((8, width), F32)],
        compiler_params=pltpu.CompilerParams(
            dimension_semantics=("parallel", "arbitrary"), vmem_limit_bytes=VMEM_LIMIT),
        name="rg_lru",
    )(p3, p3, cw, vec, wa, wx)


def _rwkv_body(r_ref, k_ref, v_ref, wa_ref, g_ref, par_ref, mixwa_ref, wup_ref, aup_ref, o_ref,
               xs_ref, s_ref):
    C = RWKV_CHUNK
    NB, tt = r_ref.shape[0], r_ref.shape[1]
    nc = tt // C

    @pl.when(pl.program_id(1) == 0)
    def _():
        xs_ref[:, 0:8, :] = jnp.zeros((xs_ref.shape[0], 8, LANES), F32)
        s_ref[...] = jnp.zeros(s_ref.shape, F32)

    lane = lax.broadcasted_iota(jnp.int32, (1, LANES), 1)
    head0 = lane < HEAD_DIM
    rc = lax.broadcasted_iota(jnp.int32, (tt, 1), 0) & (C - 1)

    par = par_ref[...]
    w0, a0, k_k, k_a = par[0:1], par[1:2], par[2:3], par[3:4]
    ln_w, ln_b, r_k = par[4:5], par[5:6], par[6:7]
    mix_r, mix_k, mix_v = par[7:8], par[8:9], par[9:10]

    def shift_lerp(ref, b, mix, slot):
        s = ref[b]
        xs_ref[4 * b + slot, 8:8 + tt, :] = s
        prev = xs_ref[4 * b + slot, 7:7 + tt, :]
        xs_ref[4 * b + slot, 0:8, :] = s[tt - 8:tt, :]
        return s + mix * (prev - s)

    def headsum(x):
        s0 = jnp.sum(jnp.where(head0, x, 0.0), axis=-1, keepdims=True)
        s1 = jnp.sum(jnp.where(head0, 0.0, x), axis=-1, keepdims=True)
        return jnp.where(head0, s0, s1)

    def chunks(x):
        return [x[ch * C:(ch + 1) * C] for ch in range(nc)]

    def front(b):
        r = shift_lerp(r_ref, b, mix_r, 0)
        k = shift_lerp(k_ref, b, mix_k, 1)
        v = shift_lerp(v_ref, b, mix_v, 2)
        wa = shift_lerp(wa_ref, b, mixwa_ref[...], 3)
        log_w = (-RWKV_DECAY_SCALE) * _sigmoid(w0 + _dot(jnp.tanh(wa), wup_ref[...]))
        a = _sigmoid(a0 + _dot(wa, aup_ref[...]))
        kk = k * k_k
        kk = kk * lax.rsqrt(jnp.maximum(headsum(kk * kk), 1e-24))
        k = k * (1.0 + (a - 1.0) * k_a)
        be = kk * a
        c = log_w
        d = 1
        while d < C:
            c = c + jnp.where(rc >= d, pltpu.roll(c, d, 0), 0.0)
            d *= 2
        c_last = jnp.broadcast_to(c.reshape(nc, C, LANES)[:, C - 1:C, :], (nc, C, LANES)).reshape(tt, LANES)
        e_inc = jnp.exp(c)
        e_inv = jnp.exp(-c)
        e_end = jnp.exp(c_last - c)
        parts = dict(a=-kk * jnp.exp(c - log_w), r=r * e_inc, kt=k * e_inv, bt=be * e_inv,
                     kd=k * e_end, bd=be * e_end, v=v, gamma_end=jnp.exp(c_last))
        return {name: chunks(val) for name, val in parts.items()}, headsum(r * k * r_k) * v

    fronts = [front(b) for b in range(NB)]
    gather = lambda name: [x for f, _ in fronts for x in f[name]]

    ti = lax.broadcasted_iota(jnp.int32, (C, LANES), 0)
    si = lax.broadcasted_iota(jnp.int32, (C, LANES), 1) & (C - 1)
    strict = ti > si
    incl = ti >= si
    same16 = (ti >> 4) == (si >> 4)
    same32 = (ti >> 5) == (si >> 5)
    eye = (ti == si).astype(F32)
    same_head = ((lax.broadcasted_iota(jnp.int32, (LANES, LANES), 0) >> 6)
                 == (lax.broadcasted_iota(jnp.int32, (LANES, LANES), 1) >> 6))
    head0_b = jnp.broadcast_to(head0, (C, LANES))

    def embed(x):
        xb = x.astype(BF16)
        zero = jnp.zeros_like(xb)
        return jnp.concatenate([jnp.where(head0_b, xb, zero), jnp.where(head0_b, zero, xb)], axis=0)

    def each(f, *lists):
        return [f(*xs) for xs in zip(*lists)]

    lo = lambda x: x[:, 0:LANES]
    hi = lambda x: x[:, LANES:2 * LANES]
    side = lambda x, y: jnp.concatenate([x, y], axis=1)

    a_c = gather('a')
    r_c = gather('r')
    k_e = each(embed, gather('kt'))
    b_e = each(embed, gather('bt'))
    kd_c = gather('kd')
    bd_c = gather('bd')
    v_c = gather('v')
    v_e = each(embed, v_c)
    gamma_end = gather('gamma_end')

    p = each(lambda aa, rr, kt, bt: _dot(jnp.concatenate([aa, rr], axis=0),
                                         jnp.concatenate([kt, bt], axis=0), 1, 1), a_c, r_c, k_e, b_e)
    a_ak = each(lambda x: jnp.where(strict, x[0:C, 0:LANES], 0.0), p)
    n = each(lambda x: jnp.where(strict, x[0:C, LANES:2 * LANES], 0.0), p)
    a_rk = each(lambda x: jnp.where(incl, x[C:2 * C, 0:LANES], 0.0), p)
    a_rb = each(lambda x: jnp.where(incl, x[C:2 * C, LANES:2 * LANES], 0.0), p)
    q = each(lambda x: jnp.where(same16, x, 0.0), n)
    t = each(lambda x: eye + x, q)
    q = each(lambda x: _dot(x, embed(x)), q)
    for _ in range(2):
        qt = each(lambda x, y: _dot(x, side(embed(y), embed(x))), q, t)
        t = each(lambda x, y: x + lo(y), t, qt)
        q = each(hi, qt)
    t = each(lambda x, y: x + _dot(y, embed(x)), t, q)
    tn = each(lambda x, y: _dot(x, embed(jnp.where(same32 & jnp.logical_not(same16), y, 0.0))), t, n)
    t = each(lambda x, y: x + _dot(y, embed(x)), t, tn)
    tn = each(lambda x, y: _dot(x, embed(jnp.where(same32, 0.0, y))), t, n)
    t = each(lambda x, y: x + _dot(y, embed(x)), t, tn)
    av = each(lambda x, y, vv: _dot(jnp.concatenate([x, y], axis=0), vv), a_ak, a_rk, v_e)
    taw = each(lambda x, aa, w: _dot(x, side(embed(aa), embed(w[0:C]))), t, a_c, av)
    ry = each(lambda x, y: _dot(x, side(embed(lo(y)), embed(hi(y)))), a_rb, taw)
    rq = each(lambda rr, x: rr + lo(x), r_c, ry)
    y0 = each(lambda w, x: w[C:2 * C] + hi(x), av, ry)
    g = each(lambda x, bd: jnp.where(same_head, _dot(lo(x), bd, 0, 0), 0.0), taw, bd_c)
    s0 = each(lambda vv, x, kd, bd: jnp.where(same_head, _dot(jnp.concatenate([vv, hi(x)], axis=0),
                                                              jnp.concatenate([kd, bd], axis=0), 0, 0), 0.0),
              v_c, taw, kd_c, bd_c)

    states = [s_ref[b] for b in range(NB)]
    ys = [[] for _ in range(NB)]
    for ch in range(nc):
        for b in range(NB):
            i, s = b * nc + ch, states[b]
            ys[b].append(_dot(rq[i], s, 1, 1) + y0[i])
            states[b] = s * gamma_end[i][C - 1:C, :] + _dot(s, g[i]) + s0[i]
    for b in range(NB):
        s_ref[b] = states[b]
        y = jnp.concatenate(ys[b], axis=0)
        mu = headsum(y) * (1.0 / HEAD_DIM)
        yc = y - mu
        var = headsum(yc * yc) * (1.0 / HEAD_DIM)
        yn = yc * lax.rsqrt(var + RWKV_GN_EPS) * ln_w + ln_b
        o_ref[b] = ((yn + fronts[b][1]) * _silu(g_ref[b])).astype(o_ref.dtype)


def _rwkv(p3, par, mixwa, wup, aup, l, col0, gcol0, npairs, tt):
    B, T, _ = p3.shape
    col = lambda off: pl.BlockSpec((B, tt, LANES), lambda j, t: (0, t, off + j))
    fixed = lambda off: pl.BlockSpec((B, tt, LANES), lambda j, t: (0, t, off))
    return pl.pallas_call(
        _rwkv_body,
        grid=(npairs, T // tt),
        in_specs=[col(col0), col(col0 + npairs), col(col0 + 2 * npairs), fixed(col0 + 3 * npairs),
                  col(gcol0),
                  _layer(l, 16, LANES)(lambda j, t: (0, j)),
                  _layer(l, 1, LANES)(lambda j, t: (0, 0)),
                  _layer(l, LANES, LANES)(lambda j, t: (0, j)),
                  _layer(l, LANES, LANES)(lambda j, t: (0, j))],
        out_specs=pl.BlockSpec((B, tt, LANES), lambda j, t: (0, t, j)),
        out_shape=jax.ShapeDtypeStruct((B, T, npairs * LANES), BF16),
        scratch_shapes=[pltpu.VMEM((4 * B, tt + 8, LANES), F32), pltpu.VMEM((B, LANES, LANES), F32)],
        compiler_params=pltpu.CompilerParams(
            dimension_semantics=("parallel", "arbitrary"), vmem_limit_bytes=VMEM_LIMIT),
        name="rwkv7",
    )(p3, p3, p3, p3, p3, par, mixwa, wup, aup)


def _moba_body(q_ref, k_ref, v_ref, g_ref, bown_ref, bprev_ref, o_ref, kmean_ref, kb_ref, vt_ref, acc_ref,
               sc_ref, p_ref):
    BS = MOBA_BLOCK
    NB, T = k_ref.shape[0], k_ref.shape[1]
    nb = T // BS
    i = pl.program_id(1)
    lane = lax.broadcasted_iota(jnp.int32, (1, LANES), 1)
    head0 = lane < HEAD_DIM
    streams = [(b, h) for b in range(NB) for h in range(2)]

    def each(f, *lists):
        return [f(*xs) for xs in zip(*lists)]

    VR = MOBA_VROWS

    @pl.when(i == 0)
    def _():
        for b in range(NB):
            kmean_ref[b] = jnp.mean(k_ref[b].reshape(nb, BS, LANES), axis=1)

        def fill(j, _):
            rows = pl.ds(pl.multiple_of(j * BS, BS), BS)
            ones = jnp.ones((VR - HEAD_DIM, BS), BF16)
            for b in range(NB):
                kb_ref[b, rows, :] = k_ref[b, rows, :].astype(BF16)
                vt = v_ref[b, rows, :].T.astype(BF16)
                vt_ref[b, j] = jnp.concatenate([vt[0:HEAD_DIM], ones, vt[HEAD_DIM:2 * HEAD_DIM], ones], axis=0)
            return 0

        lax.fori_loop(0, nb, fill, 0)

    qh = []
    for b in range(NB):
        q = q_ref[b]
        qh += [jnp.where(head0, q, 0.0), jnp.where(head0, 0.0, q)]
    q_scaled = [x * (HEAD_DIM ** -0.5 * LOG2_E) for x in qh]
    qs = [x.astype(BF16) for x in q_scaled]

    def scores(j):
        rows = pl.ds(pl.multiple_of(j * BS, BS), BS)
        return [_dot(kb_ref[b, rows, :], qs[s], 1, 1) for s, (b, _) in enumerate(streams)]

    def weighted(j, s, p):
        b, h = streams[s]
        return _dot(vt_ref[b, j, h * VR:(h + 1) * VR, :], p)

    def acc_at(s):
        b, h = streams[s]
        return b, slice(h * VR, (h + 1) * VR)

    def split(x):
        hi = x.astype(BF16)
        return hi, (x - hi.astype(F32)).astype(BF16)

    prev_blk = jnp.maximum(i - 1, 0)
    own_rows = pl.ds(pl.multiple_of(i * BS, BS), BS)
    kmean_parts = [split(kmean_ref[b]) for b in range(NB)]
    own = [_dot(jnp.concatenate([kb_ref[b, own_rows, :], kmean_parts[b][0], kmean_parts[b][1]], axis=0), qs[s], 1, 1)
           for s, (b, _) in enumerate(streams)]
    s_own = each(lambda x, bh: x[0:BS] + bown_ref[bh[1]], own, streams)
    gate = [x[BS:BS + nb] + x[BS + nb:BS + 2 * nb]
            + _dot(kmean_parts[b][0], (xq - q.astype(F32)).astype(BF16), 1, 1)
            for x, xq, q, (b, _) in zip(own, q_scaled, qs, streams)]
    s_prev = each(lambda x, bh: x + bprev_ref[bh[1]], scores(prev_blk), streams)
    for s, x in enumerate(scores(0)):
        sc_ref[s] = x

    m = each(lambda x: jnp.max(x, axis=0, keepdims=True), s_own)
    for s in range(len(streams)):
        b, rows = acc_at(s)
        acc_ref[b, rows, :] = weighted(i, s, jnp.exp2(s_own[s] - m[s]))

    blk = lax.broadcasted_iota(jnp.int32, (nb, 1), 0).astype(F32)
    i_f = i.astype(F32)
    gate = each(lambda x: jnp.where(blk < i_f, x, -jnp.inf), gate)
    sel = []
    for rnk in range(MOBA_TOPK):
        top = each(lambda x: jnp.max(x, axis=0, keepdims=True), gate)
        idx = each(lambda x, tp: jnp.min(jnp.where(x == tp, blk, float(nb)), axis=0, keepdims=True), gate, top)
        gate = each(lambda x, ix: jnp.where(blk == ix, -jnp.inf, x), gate, idx)
        sel.append(each(lambda ix: jnp.where(rnk < i, ix, -5.0), idx))

    def chosen(s, j):
        j_f = j.astype(F32)
        return (sel[0][s] == j_f) | (sel[1][s] == j_f) | (sel[2][s] == j_f)

    def softmax_step(j, sc, m, valid):
        m_new, alpha = [], []
        for s in range(len(streams)):
            take = chosen(s, j) & valid
            top = jnp.max(sc[s], axis=0, keepdims=True)
            mn = jnp.maximum(m[s], jnp.where(take, top, -jnp.inf))
            p_ref[s] = jnp.exp2(sc[s] - jnp.where(take, mn, jnp.inf)).astype(BF16)
            m_new.append(mn)
            alpha.append(jnp.exp2(m[s] - mn))
        return m_new, alpha

    def accumulate(j, alpha):
        for s in range(len(streams)):
            b, rows = acc_at(s)
            acc_ref[b, rows, :] = alpha[s] * acc_ref[b, rows, :] + weighted(j, s, p_ref[s])

    m, alpha = softmax_step(prev_blk, s_prev, m, i >= 1)

    def far_block(j, st):
        m, alpha = list(st[0]), list(st[1])
        nxt = scores(jnp.minimum(j + 1, nb - 1))
        accumulate(jnp.where(j == 0, prev_blk, j - 1), alpha)
        m, alpha = softmax_step(j, [sc_ref[s] for s in range(len(streams))], m, True)
        for s, x in enumerate(nxt):
            sc_ref[s] = x
        return tuple(m), tuple(alpha)

    n_far = jnp.maximum(i - 1, 0)
    m, alpha = lax.fori_loop(0, n_far, far_block, (tuple(m), tuple(alpha)))
    accumulate(jnp.where(n_far == 0, prev_blk, n_far - 1), alpha)

    for b in range(NB):
        acc = acc_ref[b]
        out_t = jnp.concatenate([acc[h * VR:h * VR + HEAD_DIM] * (1.0 / acc[h * VR + HEAD_DIM:h * VR + HEAD_DIM + 1])
                                 for h in range(2)], axis=0)
        o_ref[b] = (out_t.T * _silu(g_ref[b])).astype(o_ref.dtype)


def _moba(p3, bown, bprev, col0, gcol0, npairs):
    B, T, _ = p3.shape
    BS = MOBA_BLOCK
    nb = T // BS
    return pl.pallas_call(
        _moba_body,
        grid=(npairs, nb),
        in_specs=[pl.BlockSpec((B, BS, LANES), lambda j, i: (0, i, col0 + j)),
                  pl.BlockSpec((B, T, LANES), lambda j, i: (0, 0, col0 + npairs + j)),
                  pl.BlockSpec((B, T, LANES), lambda j, i: (0, 0, col0 + 2 * npairs + j)),
                  pl.BlockSpec((B, BS, LANES), lambda j, i: (0, i, gcol0 + j)),
                  pl.BlockSpec((2, BS, BS), lambda j, i: (j, 0, 0)),
                  pl.BlockSpec((2, BS, BS), lambda j, i: (j, 0, 0))],
        out_specs=pl.BlockSpec((B, BS, LANES), lambda j, i: (0, i, j)),
        out_shape=jax.ShapeDtypeStruct((B, T, npairs * LANES), BF16),
        scratch_shapes=[pltpu.VMEM((B, nb, LANES), F32), pltpu.VMEM((B, T, LANES), BF16),
                        pltpu.VMEM((B, nb, 2 * MOBA_VROWS, BS), BF16), pltpu.VMEM((B, 2 * MOBA_VROWS, BS), F32),
                        pltpu.VMEM((2 * B, BS, BS), F32), pltpu.VMEM((2 * B, BS, BS), BF16)],
        compiler_params=pltpu.CompilerParams(
            dimension_semantics=("parallel", "arbitrary"), vmem_limit_bytes=VMEM_LIMIT),
        name="moba",
    )(p3, p3, p3, p3, bown, bprev)


def _rel_bucket_of(dist):
    max_exact = REL_BUCKETS // 2
    large = max_exact + (jnp.log(jnp.maximum(dist, 1).astype(F32) / max_exact)
                         / math.log(REL_MAX_DIST / max_exact) * (REL_BUCKETS - max_exact)).astype(jnp.int32)
    large = jnp.minimum(large, REL_BUCKETS - 1)
    return jnp.where(dist < max_exact, dist, large)


def _toeplitz_kq(tab):
    H, L = tab.shape
    BS = L // 2
    flat = jnp.tile(jnp.roll(tab, -1, axis=1), (1, BS))[:, :BS * (L - 1)]
    return flat.reshape(H, BS, L - 1)[:, :, BS - 1:2 * BS - 1]


def _moba_bias_tables(rel_bias):
    BS = MOBA_BLOCK
    per_dist = rel_bias.astype(F32)[_rel_bucket_of(jnp.arange(2 * BS))].T
    per_dist = (per_dist - per_dist[:, 2 * BS - 1:2 * BS]) * LOG2_E
    prev = _toeplitz_kq(per_dist)
    ki = np.arange(BS)[:, None]
    qi = np.arange(BS)[None, :]
    own = jnp.where(ki <= qi, _toeplitz_kq(jnp.roll(per_dist, BS, axis=1)), -jnp.inf)
    return own, prev


def _block_diag(w):
    d, g, n, _ = w.shape
    return jnp.einsum('lgij,gh->lgihj', w, jnp.eye(g, dtype=w.dtype)).reshape(d, g * n, g * n)


def kernel(x, norm_w, w_in, w_out, lru_conv_w, lru_conv_b, lru_gate_a_w, lru_gate_a_b, lru_gate_x_w, lru_gate_x_b, lru_lambda, rwkv_mix, rwkv_w0, rwkv_w_up, rwkv_a0, rwkv_a_up, rwkv_k_k, rwkv_k_a, rwkv_r_k, rwkv_ln_w, rwkv_ln_b, rel_bias, final_norm_w):
    B, T, D = x.shape
    depth = w_in.shape[0]
    lru_w = lru_conv_w.shape[2]
    rwkv_w = rwkv_w0.shape[1]
    moba_w = rel_bias.shape[1] * HEAD_DIM
    rwkv_pairs = rwkv_w // LANES
    moba_pairs = moba_w // LANES
    rwkv_col = (2 * lru_w) // LANES
    rwkv_gate_col = rwkv_col + (3 * rwkv_w + 2 * RWKV_LORA) // LANES
    moba_col = rwkv_gate_col + rwkv_w // LANES
    moba_gate_col = moba_col + (3 * moba_w) // LANES
    n_cols = w_in.shape[2]
    assert (moba_gate_col + moba_w // LANES) * LANES == n_cols
    assert T % MOBA_BLOCK == 0 and 2 * RWKV_LORA == LANES

    M = B * T
    tm_in = 1024
    tn_in = n_cols // 3
    tm_out = 512
    tt_lru = 512
    tt_rwkv = 1024

    bown, bprev = _moba_bias_tables(rel_bias)
    w_in_b = w_in.astype(BF16)
    w_out_b = w_out.astype(BF16)
    lru_vec = jnp.stack([lru_conv_b, lru_gate_a_b, lru_gate_x_b, lru_lambda] + [jnp.zeros_like(lru_lambda)] * 4,
                        axis=1)
    lru_wa = _block_diag(lru_gate_a_w).astype(BF16)
    lru_wx = _block_diag(lru_gate_x_w).astype(BF16)
    mix = rwkv_mix
    rwkv_par = jnp.stack([rwkv_w0, rwkv_a0, rwkv_k_k, rwkv_k_a, rwkv_ln_w, rwkv_ln_b,
                          rwkv_r_k.reshape(depth, rwkv_w), mix[:, 0:rwkv_w], mix[:, rwkv_w:2 * rwkv_w],
                          mix[:, 2 * rwkv_w:3 * rwkv_w]] + [jnp.zeros_like(rwkv_w0)] * 6, axis=1)
    mixwa = mix[:, 3 * rwkv_w:].reshape(depth, 1, LANES)
    zeros_lora = jnp.zeros_like(rwkv_w_up)
    wup = jnp.concatenate([rwkv_w_up, zeros_lora], axis=1).astype(BF16)
    aup = jnp.concatenate([zeros_lora, rwkv_a_up], axis=1).astype(BF16)

    x2 = x.reshape(M, D)
    h = _norm_cast(x2, norm_w[0].reshape(1, D), tm_out)
    for l in range(depth):
        p3 = _in_proj(h, w_in_b, l, tm_in, tn_in).reshape(B, T, n_cols)
        ya = _lru(p3, lru_conv_w, lru_vec, lru_wa, lru_wx, l, lru_w, tt_lru)
        yb = _rwkv(p3, rwkv_par, mixwa, wup, aup, l, rwkv_col, rwkv_gate_col, rwkv_pairs, tt_rwkv)
        yc = _moba(p3, bown, bprev, moba_col, moba_gate_col, moba_pairs)
        final = l == depth - 1
        nw = (final_norm_w if final else norm_w[l + 1]).reshape(1, D)
        res = _out_proj(x2, ya.reshape(M, lru_w), yb.reshape(M, rwkv_w), yc.reshape(M, moba_w),
                        w_out_b, l, nw, tm_out, final)
        if final:
            return res.reshape(B, T, D)
        x2, h = res
```

```python
import functools
import math

import numpy as np
import jax
import jax.numpy as jnp
from jax import lax
from jax.experimental import pallas as pl
from jax.experimental.pallas import tpu as pltpu

F32 = jnp.float32
BF16 = jnp.bfloat16

LANES = 128
HEAD_DIM = 64
NORM_EPS = 1e-6
LRU_CONV = 4
LRU_C = 8.0
RWKV_LORA = 64
RWKV_DECAY_SCALE = math.exp(-0.5)
RWKV_GN_EPS = 64e-5
RWKV_CHUNK = 64
MOBA_BLOCK = 256
MOBA_TOPK = 3
MOBA_VROWS = 80
REL_BUCKETS = 32
REL_MAX_DIST = 128
LOG2_E = math.log2(math.e)
VMEM_LIMIT = 60 * 1024 * 1024


def _dot(a, b, ca=1, cb=0):
    return lax.dot_general(a.astype(BF16), b.astype(BF16), (((ca,), (cb,)), ((), ())),
                           preferred_element_type=F32)


def _sigmoid(x):
    return 0.5 * jnp.tanh(0.5 * x) + 0.5


def _silu(x):
    return x * _sigmoid(x)


def _rms_norm(x, w):
    ms = jnp.mean(x * x, axis=-1, keepdims=True)
    return x * lax.rsqrt(ms + NORM_EPS) * w


def _norm_body(x_ref, nw_ref, wi_ref, wo_ref, h_ref, wib_ref, wob_ref):
    h_ref[...] = _rms_norm(x_ref[...], nw_ref[...]).astype(BF16)
    wib_ref[...] = wi_ref[...].astype(BF16)
    wob_ref[...] = wo_ref[...].astype(BF16)


def _slab_rows(D, steps):
    rows = D // steps
    assert rows * steps == D and rows % 16 == 0
    return rows


def _norm_cast(x2, nw, w_in, w_out, tm):
    M, D = x2.shape
    N = w_in.shape[2]
    rows = _slab_rows(D, M // tm)
    return pl.pallas_call(
        _norm_body,
        grid=(M // tm,),
        in_specs=[pl.BlockSpec((tm, D), lambda i: (i, 0)), pl.BlockSpec((1, D), lambda i: (0, 0)),
                  _layer(0, rows, N)(lambda i: (i, 0)), _layer(0, rows, D)(lambda i: (i, 0))],
        out_specs=[pl.BlockSpec((tm, D), lambda i: (i, 0)), pl.BlockSpec((rows, N), lambda i: (i, 0)),
                   pl.BlockSpec((rows, D), lambda i: (i, 0))],
        out_shape=[jax.ShapeDtypeStruct((M, D), BF16), jax.ShapeDtypeStruct((D, N), BF16),
                   jax.ShapeDtypeStruct((D, D), BF16)],
        compiler_params=pltpu.CompilerParams(dimension_semantics=("parallel",), vmem_limit_bytes=VMEM_LIMIT),
        name="norm_cast",
    )(x2, nw, w_in, w_out)


def _layer(l, *block):
    def spec(tail):
        return pl.BlockSpec((None,) + block, lambda *g: (l,) + tail(*g))
    return spec


def _inproj_body(h_ref, w_ref, *rest):
    if len(rest) == 3:
        wn_ref, o_ref, wnb_ref = rest
        wnb_ref[...] = wn_ref[...].astype(BF16)
    else:
        (o_ref,) = rest
    o_ref[...] = jnp.dot(h_ref[...], w_ref[...], preferred_element_type=F32)


def _in_proj(h, w, w_all, l_next, tm, tn):
    M, D = h.shape
    N = w.shape[1]
    in_specs = [pl.BlockSpec((tm, D), lambda j, i: (i, 0)), pl.BlockSpec((D, tn), lambda j, i: (0, j))]
    out_specs = pl.BlockSpec((tm, tn), lambda j, i: (i, j))
    out_shape = jax.ShapeDtypeStruct((M, N), F32)
    args = (h, w)
    if l_next is not None:
        rows = _slab_rows(D, M // tm)
        in_specs.append(_layer(l_next, rows, tn)(lambda j, i: (i, j)))
        out_specs = [out_specs, pl.BlockSpec((rows, tn), lambda j, i: (i, j))]
        out_shape = [out_shape, jax.ShapeDtypeStruct((D, N), BF16)]
        args += (w_all,)
    return pl.pallas_call(
        _inproj_body,
        grid=(N // tn, M // tm),
        in_specs=in_specs,
        out_specs=out_specs,
        out_shape=out_shape,
        compiler_params=pltpu.CompilerParams(
            dimension_semantics=("parallel", "parallel"), vmem_limit_bytes=VMEM_LIMIT),
        name="in_proj" if l_next is not None else "in_proj_last",
    )(*args)


def _outproj_body(x_ref, ya_ref, yb_ref, yc_ref, w_ref, nw_ref, *rest, final):
    wa, wb = ya_ref.shape[1], yb_ref.shape[1]
    acc = x_ref[...]
    acc = acc + _dot(ya_ref[...], w_ref[0:wa, :])
    acc = acc + _dot(yb_ref[...], w_ref[wa:wa + wb, :])
    acc = acc + _dot(yc_ref[...], w_ref[wa + wb:, :])
    normed = _rms_norm(acc, nw_ref[...])
    if final:
        rest[0][...] = normed
    else:
        wn_ref, x_out_ref, h_ref, wnb_ref = rest
        x_out_ref[...] = acc
        h_ref[...] = normed.astype(BF16)
        wnb_ref[...] = wn_ref[...].astype(BF16)


def _out_proj(x2, ya, yb, yc, w, nw, w_all, l_next, tm):
    M, D = x2.shape
    final = l_next is None
    row = lambda width: pl.BlockSpec((tm, width), lambda i: (i, 0))
    full = lambda a: pl.BlockSpec(a.shape, lambda i: (0, 0))
    in_specs = [row(D), row(ya.shape[1]), row(yb.shape[1]), row(yc.shape[1]), full(w), full(nw)]
    args = (x2, ya, yb, yc, w, nw)
    if final:
        out_specs, out_shape = row(D), jax.ShapeDtypeStruct((M, D), F32)
    else:
        rows = _slab_rows(D, M // tm)
        in_specs.append(_layer(l_next, rows, D)(lambda i: (i, 0)))
        args += (w_all,)
        out_specs = [row(D), row(D), pl.BlockSpec((rows, D), lambda i: (i, 0))]
        out_shape = [jax.ShapeDtypeStruct((M, D), F32), jax.ShapeDtypeStruct((M, D), BF16),
                     jax.ShapeDtypeStruct((D, D), BF16)]
    return pl.pallas_call(
        functools.partial(_outproj_body, final=final),
        grid=(M // tm,),
        in_specs=in_specs,
        out_specs=out_specs,
        out_shape=out_shape,
        compiler_params=pltpu.CompilerParams(
            dimension_semantics=("parallel",), vmem_limit_bytes=VMEM_LIMIT),
        name="out_proj_final" if final else "out_proj",
    )(*args)


def _lru_body(xa_ref, ga_ref, cw_ref, vec_ref, wa_ref, wx_ref, o_ref, xbuf, hcar):
    tt = xa_ref.shape[1]
    W = xa_ref.shape[2]

    @pl.when(pl.program_id(1) == 0)
    def _():
        xbuf[0:8, :] = jnp.zeros((8, W), F32)
        hcar[...] = jnp.zeros((8, W), F32)

    xa = xa_ref[0]
    xbuf[8:8 + tt, :] = xa
    xc = vec_ref[0:1, :] + cw_ref[LRU_CONV - 1:LRU_CONV, :] * xa
    for s in range(1, LRU_CONV):
        xc = xc + cw_ref[LRU_CONV - 1 - s:LRU_CONV - s, :] * xbuf[8 - s:8 - s + tt, :]
    xbuf[0:8, :] = xa[tt - 8:tt, :]

    r = _sigmoid(_dot(xc, wa_ref[...]) + vec_ref[1:2, :])
    gi = _sigmoid(_dot(xc, wx_ref[...]) + vec_ref[2:3, :])
    lam = vec_ref[3:4, :]
    softplus_neg_lam = jnp.maximum(-lam, 0.0) + jnp.log1p(jnp.exp(-jnp.abs(lam)))
    log_a = (-LRU_C) * r * softplus_neg_lam
    a = jnp.exp(log_a)
    one_minus_a2 = -jnp.tanh(log_a) * (a * a + 1.0)
    b = jnp.where(one_minus_a2 > 0.0, one_minus_a2 * lax.rsqrt(one_minus_a2), 0.0) * (gi * xc)

    row = lax.broadcasted_iota(jnp.int32, (tt, 1), 0)
    d = 1
    while d < tt:
        keep = row >= d
        a_sh = pltpu.roll(a, d, 0)
        b_sh = pltpu.roll(b, d, 0)
        b = b + jnp.where(keep, a * b_sh, 0.0)
        a = jnp.where(keep, a * a_sh, a)
        d *= 2
    h = b + a * hcar[0:1, :]
    hcar[...] = jnp.broadcast_to(h[tt - 1:tt, :], (8, W))
    o_ref[0] = (h * _silu(ga_ref[0])).astype(o_ref.dtype)


def _lru(p3, cw, vec, wa, wx, l, width, tt):
    B, T, _ = p3.shape
    full = lambda a: _layer(l, *a.shape[1:])(lambda b, t: (0, 0))
    return pl.pallas_call(
        _lru_body,
        grid=(B, T // tt),
        in_specs=[pl.BlockSpec((1, tt, width), lambda b, t: (b, t, 0)),
                  pl.BlockSpec((1, tt, width), lambda b, t: (b, t, 1)),
                  full(cw), full(vec), full(wa), full(wx)],
        out_specs=pl.BlockSpec((1, tt, width), lambda b, t: (b, t, 0)),
        out_shape=jax.ShapeDtypeStruct((B, T, width), BF16),
        scratch_shapes=[pltpu.VMEM((tt + 8, width), F32), pltpu.VMEM((8, width), F32)],
        compiler_params=pltpu.CompilerParams(
            dimension_semantics=("parallel", "arbitrary"), vmem_limit_bytes=VMEM_LIMIT),
        name="rg_lru",
    )(p3, p3, cw, vec, wa, wx)


def _rwkv_body(r_ref, k_ref, v_ref, wa_ref, g_ref, par_ref, mixwa_ref, wup_ref, aup_ref, o_ref,
               xs_ref, s_ref):
    C = RWKV_CHUNK
    NB, tt = r_ref.shape[0], r_ref.shape[1]
    nc = tt // C

    @pl.when(pl.program_id(1) == 0)
    def _():
        xs_ref[:, 0:8, :] = jnp.zeros((xs_ref.shape[0], 8, LANES), F32)
        s_ref[...] = jnp.zeros(s_ref.shape, F32)

    lane = lax.broadcasted_iota(jnp.int32, (1, LANES), 1)
    head0 = lane < HEAD_DIM
    rc = lax.broadcasted_iota(jnp.int32, (tt, 1), 0) & (C - 1)

    par = par_ref[...]
    w0, a0, k_k, k_a = par[0:1], par[1:2], par[2:3], par[3:4]
    ln_w, ln_b, r_k = par[4:5], par[5:6], par[6:7]
    mix_r, mix_k, mix_v = par[7:8], par[8:9], par[9:10]

    def shift_lerp(ref, b, mix, slot):
        s = ref[b]
        xs_ref[4 * b + slot, 8:8 + tt, :] = s
        prev = xs_ref[4 * b + slot, 7:7 + tt, :]
        xs_ref[4 * b + slot, 0:8, :] = s[tt - 8:tt, :]
        return s + mix * (prev - s)

    def headsum(x):
        s0 = jnp.sum(jnp.where(head0, x, 0.0), axis=-1, keepdims=True)
        s1 = jnp.sum(jnp.where(head0, 0.0, x), axis=-1, keepdims=True)
        return jnp.where(head0, s0, s1)

    def chunks(x):
        return [x[ch * C:(ch + 1) * C] for ch in range(nc)]

    def front(b):
        r = shift_lerp(r_ref, b, mix_r, 0)
        k = shift_lerp(k_ref, b, mix_k, 1)
        v = shift_lerp(v_ref, b, mix_v, 2)
        wa = shift_lerp(wa_ref, b, mixwa_ref[...], 3)
        log_w = (-RWKV_DECAY_SCALE) * _sigmoid(w0 + _dot(jnp.tanh(wa), wup_ref[...]))
        a = _sigmoid(a0 + _dot(wa, aup_ref[...]))
        kk = k * k_k
        kk = kk * lax.rsqrt(jnp.maximum(headsum(kk * kk), 1e-24))
        k = k * (1.0 + (a - 1.0) * k_a)
        be = kk * a
        c = log_w
        d = 1
        while d < C:
            c = c + jnp.where(rc >= d, pltpu.roll(c, d, 0), 0.0)
            d *= 2
        c_last = jnp.broadcast_to(c.reshape(nc, C, LANES)[:, C - 1:C, :], (nc, C, LANES)).reshape(tt, LANES)
        e_inc = jnp.exp(c)
        e_inv = jnp.exp(-c)
        e_end = jnp.exp(c_last - c)
        parts = dict(a=-kk * jnp.exp(c - log_w), r=r * e_inc, kt=k * e_inv, bt=be * e_inv,
                     kd=k * e_end, bd=be * e_end, v=v, gamma_end=jnp.exp(c_last))
        return {name: chunks(val) for name, val in parts.items()}, headsum(r * k * r_k) * v

    fronts = [front(b) for b in range(NB)]
    gather = lambda name: [x for f, _ in fronts for x in f[name]]

    ti = lax.broadcasted_iota(jnp.int32, (C, LANES), 0)
    si = lax.broadcasted_iota(jnp.int32, (C, LANES), 1) & (C - 1)
    strict = ti > si
    incl = ti >= si
    same16 = (ti >> 4) == (si >> 4)
    same32 = (ti >> 5) == (si >> 5)
    eye = (ti == si).astype(F32)
    same_head = ((lax.broadcasted_iota(jnp.int32, (LANES, LANES), 0) >> 6)
                 == (lax.broadcasted_iota(jnp.int32, (LANES, LANES), 1) >> 6))
    head0_b = jnp.broadcast_to(head0, (C, LANES))

    def embed(x):
        xb = x.astype(BF16)
        zero = jnp.zeros_like(xb)
        return jnp.concatenate([jnp.where(head0_b, xb, zero), jnp.where(head0_b, zero, xb)], axis=0)

    def each(f, *lists):
        return [f(*xs) for xs in zip(*lists)]

    lo = lambda x: x[:, 0:LANES]
    hi = lambda x: x[:, LANES:2 * LANES]
    side = lambda x, y: jnp.concatenate([x, y], axis=1)

    a_c = gather('a')
    r_c = gather('r')
    k_e = each(embed, gather('kt'))
    b_e = each(embed, gather('bt'))
    kd_c = gather('kd')
    bd_c = gather('bd')
    v_c = gather('v')
    v_e = each(embed, v_c)
    gamma_end = gather('gamma_end')

    p = each(lambda aa, rr, kt, bt: _dot(jnp.concatenate([aa, rr], axis=0),
                                         jnp.concatenate([kt, bt], axis=0), 1, 1), a_c, r_c, k_e, b_e)
    a_ak = each(lambda x: jnp.where(strict, x[0:C, 0:LANES], 0.0), p)
    n = each(lambda x: jnp.where(strict, x[0:C, LANES:2 * LANES], 0.0), p)
    a_rk = each(lambda x: jnp.where(incl, x[C:2 * C, 0:LANES], 0.0), p)
    a_rb = each(lambda x: jnp.where(incl, x[C:2 * C, LANES:2 * LANES], 0.0), p)
    q = each(lambda x: jnp.where(same16, x, 0.0), n)
    t = each(lambda x: eye + x, q)
    q = each(lambda x: _dot(x, embed(x)), q)
    for _ in range(2):
        qt = each(lambda x, y: _dot(x, side(embed(y), embed(x))), q, t)
        t = each(lambda x, y: x + lo(y), t, qt)
        q = each(hi, qt)
    t = each(lambda x, y: x + _dot(y, embed(x)), t, q)
    tn = each(lambda x, y: _dot(x, embed(jnp.where(same32 & jnp.logical_not(same16), y, 0.0))), t, n)
    t = each(lambda x, y: x + _dot(y, embed(x)), t, tn)
    tn = each(lambda x, y: _dot(x, embed(jnp.where(same32, 0.0, y))), t, n)
    t = each(lambda x, y: x + _dot(y, embed(x)), t, tn)
    av = each(lambda x, y, vv: _dot(jnp.concatenate([x, y], axis=0), vv), a_ak, a_rk, v_e)
    taw = each(lambda x, aa, w: _dot(x, side(embed(aa), embed(w[0:C]))), t, a_c, av)
    ry = each(lambda x, y: _dot(x, side(embed(lo(y)), embed(hi(y)))), a_rb, taw)
    rq = each(lambda rr, x: rr + lo(x), r_c, ry)
    y0 = each(lambda w, x: w[C:2 * C] + hi(x), av, ry)
    g = each(lambda x, bd: jnp.where(same_head, _dot(lo(x), bd, 0, 0), 0.0), taw, bd_c)
    s0 = each(lambda vv, x, kd, bd: jnp.where(same_head, _dot(jnp.concatenate([vv, hi(x)], axis=0),
                                                              jnp.concatenate([kd, bd], axis=0), 0, 0), 0.0),
              v_c, taw, kd_c, bd_c)

    states = [s_ref[b] for b in range(NB)]
    ys = [[] for _ in range(NB)]
    for ch in range(nc):
        for b in range(NB):
            i, s = b * nc + ch, states[b]
            ys[b].append(_dot(rq[i], s, 1, 1) + y0[i])
            states[b] = s * gamma_end[i][C - 1:C, :] + _dot(s, g[i]) + s0[i]
    for b in range(NB):
        s_ref[b] = states[b]
        y = jnp.concatenate(ys[b], axis=0)
        mu = headsum(y) * (1.0 / HEAD_DIM)
        yc = y - mu
        var = headsum(yc * yc) * (1.0 / HEAD_DIM)
        yn = yc * lax.rsqrt(var + RWKV_GN_EPS) * ln_w + ln_b
        o_ref[b] = ((yn + fronts[b][1]) * _silu(g_ref[b])).astype(o_ref.dtype)


def _rwkv(p3, par, mixwa, wup, aup, l, col0, gcol0, npairs, tt):
    B, T, _ = p3.shape
    col = lambda off: pl.BlockSpec((B, tt, LANES), lambda j, t: (0, t, off + j))
    fixed = lambda off: pl.BlockSpec((B, tt, LANES), lambda j, t: (0, t, off))
    return pl.pallas_call(
        _rwkv_body,
        grid=(npairs, T // tt),
        in_specs=[col(col0), col(col0 + npairs), col(col0 + 2 * npairs), fixed(col0 + 3 * npairs),
                  col(gcol0),
                  _layer(l, 16, LANES)(lambda j, t: (0, j)),
                  _layer(l, 1, LANES)(lambda j, t: (0, 0)),
                  _layer(l, LANES, LANES)(lambda j, t: (0, j)),
                  _layer(l, LANES, LANES)(lambda j, t: (0, j))],
        out_specs=pl.BlockSpec((B, tt, LANES), lambda j, t: (0, t, j)),
        out_shape=jax.ShapeDtypeStruct((B, T, npairs * LANES), BF16),
        scratch_shapes=[pltpu.VMEM((4 * B, tt + 8, LANES), F32), pltpu.VMEM((B, LANES, LANES), F32)],
        compiler_params=pltpu.CompilerParams(
            dimension_semantics=("parallel", "arbitrary"), vmem_limit_bytes=VMEM_LIMIT),
        name="rwkv7",
    )(p3, p3, p3, p3, p3, par, mixwa, wup, aup)


def _moba_body(q_ref, k_ref, v_ref, g_ref, bown_ref, bprev_ref, o_ref, kmean_ref, kb_ref, vt_ref, acc_ref,
               sc_ref, p_ref):
    BS = MOBA_BLOCK
    NB, T = k_ref.shape[0], k_ref.shape[1]
    nb = T // BS
    i = pl.program_id(1)
    lane = lax.broadcasted_iota(jnp.int32, (1, LANES), 1)
    head0 = lane < HEAD_DIM
    streams = [(b, h) for b in range(NB) for h in range(2)]

    def each(f, *lists):
        return [f(*xs) for xs in zip(*lists)]

    VR = MOBA_VROWS

    @pl.when(i == 0)
    def _():
        for b in range(NB):
            kmean_ref[b] = jnp.mean(k_ref[b].reshape(nb, BS, LANES), axis=1)

        def fill(j, _):
            rows = pl.ds(pl.multiple_of(j * BS, BS), BS)
            ones = jnp.ones((VR - HEAD_DIM, BS), BF16)
            for b in range(NB):
                kb_ref[b, rows, :] = k_ref[b, rows, :].astype(BF16)
                vt = v_ref[b, rows, :].T.astype(BF16)
                vt_ref[b, j] = jnp.concatenate([vt[0:HEAD_DIM], ones, vt[HEAD_DIM:2 * HEAD_DIM], ones], axis=0)
            return 0

        lax.fori_loop(0, nb, fill, 0)

    qh = []
    for b in range(NB):
        q = q_ref[b]
        qh += [jnp.where(head0, q, 0.0), jnp.where(head0, 0.0, q)]
    q_scaled = [x * (HEAD_DIM ** -0.5 * LOG2_E) for x in qh]
    qs = [x.astype(BF16) for x in q_scaled]

    def scores(j):
        rows = pl.ds(pl.multiple_of(j * BS, BS), BS)
        return [_dot(kb_ref[b, rows, :], qs[s], 1, 1) for s, (b, _) in enumerate(streams)]

    def weighted(j, s, p):
        b, h = streams[s]
        return _dot(vt_ref[b, j, h * VR:(h + 1) * VR, :], p)

    def acc_at(s):
        b, h = streams[s]
        return b, slice(h * VR, (h + 1) * VR)

    def split(x):
        hi = x.astype(BF16)
        return hi, (x - hi.astype(F32)).astype(BF16)

    prev_blk = jnp.maximum(i - 1, 0)
    own_rows = pl.ds(pl.multiple_of(i * BS, BS), BS)
    kmean_parts = [split(kmean_ref[b]) for b in range(NB)]
    own = [_dot(jnp.concatenate([kb_ref[b, own_rows, :], kmean_parts[b][0], kmean_parts[b][1]], axis=0), qs[s], 1, 1)
           for s, (b, _) in enumerate(streams)]
    s_own = each(lambda x, bh: x[0:BS] + bown_ref[bh[1]], own, streams)
    gate = [x[BS:BS + nb] + x[BS + nb:BS + 2 * nb]
            + _dot(kmean_parts[b][0], (xq - q.astype(F32)).astype(BF16), 1, 1)
            for x, xq, q, (b, _) in zip(own, q_scaled, qs, streams)]
    s_prev = each(lambda x, bh: x + bprev_ref[bh[1]], scores(prev_blk), streams)
    for s, x in enumerate(scores(0)):
        sc_ref[s] = x

    m = each(lambda x: jnp.max(x, axis=0, keepdims=True), s_own)
    for s in range(len(streams)):
        b, rows = acc_at(s)
        acc_ref[b, rows, :] = weighted(i, s, jnp.exp2(s_own[s] - m[s]))

    blk = lax.broadcasted_iota(jnp.int32, (nb, 1), 0).astype(F32)
    i_f = i.astype(F32)
    gate = each(lambda x: jnp.where(blk < i_f, x, -jnp.inf), gate)
    sel = []
    for rnk in range(MOBA_TOPK):
        top = each(lambda x: jnp.max(x, axis=0, keepdims=True), gate)
        idx = each(lambda x, tp: jnp.min(jnp.where(x == tp, blk, float(nb)), axis=0, keepdims=True), gate, top)
        gate = each(lambda x, ix: jnp.where(blk == ix, -jnp.inf, x), gate, idx)
        sel.append(each(lambda ix: jnp.where(rnk < i, ix, -5.0), idx))

    def chosen(s, j):
        j_f = j.astype(F32)
        return (sel[0][s] == j_f) | (sel[1][s] == j_f) | (sel[2][s] == j_f)

    def softmax_step(j, sc, m, valid):
        m_new, alpha = [], []
        for s in range(len(streams)):
            take = chosen(s, j) & valid
            top = jnp.max(sc[s], axis=0, keepdims=True)
            mn = jnp.maximum(m[s], jnp.where(take, top, -jnp.inf))
            p_ref[s] = jnp.exp2(sc[s] - jnp.where(take, mn, jnp.inf)).astype(BF16)
            m_new.append(mn)
            alpha.append(jnp.exp2(m[s] - mn))
        return m_new, alpha

    def accumulate(j, alpha):
        for s in range(len(streams)):
            b, rows = acc_at(s)
            acc_ref[b, rows, :] = alpha[s] * acc_ref[b, rows, :] + weighted(j, s, p_ref[s])

    m, alpha = softmax_step(prev_blk, s_prev, m, i >= 1)

    def far_block(j, st):
        m, alpha = list(st[0]), list(st[1])
        nxt = scores(jnp.minimum(j + 1, nb - 1))
        accumulate(jnp.where(j == 0, prev_blk, j - 1), alpha)
        m, alpha = softmax_step(j, [sc_ref[s] for s in range(len(streams))], m, True)
        for s, x in enumerate(nxt):
            sc_ref[s] = x
        return tuple(m), tuple(alpha)

    n_far = jnp.maximum(i - 1, 0)
    m, alpha = lax.fori_loop(0, n_far, far_block, (tuple(m), tuple(alpha)))
    accumulate(jnp.where(n_far == 0, prev_blk, n_far - 1), alpha)

    for b in range(NB):
        acc = acc_ref[b]
        out_t = jnp.concatenate([acc[h * VR:h * VR + HEAD_DIM] * (1.0 / acc[h * VR + HEAD_DIM:h * VR + HEAD_DIM + 1])
                                 for h in range(2)], axis=0)
        o_ref[b] = (out_t.T * _silu(g_ref[b])).astype(o_ref.dtype)


def _moba(p3, bown, bprev, col0, gcol0, npairs):
    B, T, _ = p3.shape
    BS = MOBA_BLOCK
    nb = T // BS
    return pl.pallas_call(
        _moba_body,
        grid=(npairs, nb),
        in_specs=[pl.BlockSpec((B, BS, LANES), lambda j, i: (0, i, col0 + j)),
                  pl.BlockSpec((B, T, LANES), lambda j, i: (0, 0, col0 + npairs + j)),
                  pl.BlockSpec((B, T, LANES), lambda j, i: (0, 0, col0 + 2 * npairs + j)),
                  pl.BlockSpec((B, BS, LANES), lambda j, i: (0, i, gcol0 + j)),
                  pl.BlockSpec((2, BS, BS), lambda j, i: (j, 0, 0)),
                  pl.BlockSpec((2, BS, BS), lambda j, i: (j, 0, 0))],
        out_specs=pl.BlockSpec((B, BS, LANES), lambda j, i: (0, i, j)),
        out_shape=jax.ShapeDtypeStruct((B, T, npairs * LANES), BF16),
        scratch_shapes=[pltpu.VMEM((B, nb, LANES), F32), pltpu.VMEM((B, T, LANES), BF16),
                        pltpu.VMEM((B, nb, 2 * MOBA_VROWS, BS), BF16), pltpu.VMEM((B, 2 * MOBA_VROWS, BS), F32),
                        pltpu.VMEM((2 * B, BS, BS), F32), pltpu.VMEM((2 * B, BS, BS), BF16)],
        compiler_params=pltpu.CompilerParams(
            dimension_semantics=("parallel", "arbitrary"), vmem_limit_bytes=VMEM_LIMIT),
        name="moba",
    )(p3, p3, p3, p3, bown, bprev)


def _rel_bucket_of(dist):
    max_exact = REL_BUCKETS // 2
    large = max_exact + (jnp.log(jnp.maximum(dist, 1).astype(F32) / max_exact)
                         / math.log(REL_MAX_DIST / max_exact) * (REL_BUCKETS - max_exact)).astype(jnp.int32)
    large = jnp.minimum(large, REL_BUCKETS - 1)
    return jnp.where(dist < max_exact, dist, large)


def _toeplitz_kq(tab):
    H, L = tab.shape
    BS = L // 2
    flat = jnp.tile(jnp.roll(tab, -1, axis=1), (1, BS))[:, :BS * (L - 1)]
    return flat.reshape(H, BS, L - 1)[:, :, BS - 1:2 * BS - 1]


def _moba_bias_tables(rel_bias):
    BS = MOBA_BLOCK
    per_dist = rel_bias.astype(F32)[_rel_bucket_of(jnp.arange(2 * BS))].T
    per_dist = (per_dist - per_dist[:, 2 * BS - 1:2 * BS]) * LOG2_E
    prev = _toeplitz_kq(per_dist)
    ki = np.arange(BS)[:, None]
    qi = np.arange(BS)[None, :]
    own = jnp.where(ki <= qi, _toeplitz_kq(jnp.roll(per_dist, BS, axis=1)), -jnp.inf)
    return own, prev


def _block_diag(w):
    d, g, n, _ = w.shape
    return jnp.einsum('lgij,gh->lgihj', w, jnp.eye(g, dtype=w.dtype)).reshape(d, g * n, g * n)


def kernel(x, norm_w, w_in, w_out, lru_conv_w, lru_conv_b, lru_gate_a_w, lru_gate_a_b, lru_gate_x_w, lru_gate_x_b, lru_lambda, rwkv_mix, rwkv_w0, rwkv_w_up, rwkv_a0, rwkv_a_up, rwkv_k_k, rwkv_k_a, rwkv_r_k, rwkv_ln_w, rwkv_ln_b, rel_bias, final_norm_w):
    B, T, D = x.shape
    depth = w_in.shape[0]
    lru_w = lru_conv_w.shape[2]
    rwkv_w = rwkv_w0.shape[1]
    moba_w = rel_bias.shape[1] * HEAD_DIM
    rwkv_pairs = rwkv_w // LANES
    moba_pairs = moba_w // LANES
    rwkv_col = (2 * lru_w) // LANES
    rwkv_gate_col = rwkv_col + (3 * rwkv_w + 2 * RWKV_LORA) // LANES
    moba_col = rwkv_gate_col + rwkv_w // LANES
    moba_gate_col = moba_col + (3 * moba_w) // LANES
    n_cols = w_in.shape[2]
    assert (moba_gate_col + moba_w // LANES) * LANES == n_cols
    assert T % MOBA_BLOCK == 0 and 2 * RWKV_LORA == LANES

    M = B * T
    tm_in = 1024
    tn_in = n_cols // 3
    tm_out = 512
    tt_lru = 512
    tt_rwkv = 1024

    bown, bprev = _moba_bias_tables(rel_bias)
    lru_vec = jnp.stack([lru_conv_b, lru_gate_a_b, lru_gate_x_b, lru_lambda] + [jnp.zeros_like(lru_lambda)] * 4,
                        axis=1)
    lru_wa = _block_diag(lru_gate_a_w).astype(BF16)
    lru_wx = _block_diag(lru_gate_x_w).astype(BF16)
    mix = rwkv_mix
    rwkv_par = jnp.stack([rwkv_w0, rwkv_a0, rwkv_k_k, rwkv_k_a, rwkv_ln_w, rwkv_ln_b,
                          rwkv_r_k.reshape(depth, rwkv_w), mix[:, 0:rwkv_w], mix[:, rwkv_w:2 * rwkv_w],
                          mix[:, 2 * rwkv_w:3 * rwkv_w]] + [jnp.zeros_like(rwkv_w0)] * 6, axis=1)
    mixwa = mix[:, 3 * rwkv_w:].reshape(depth, 1, LANES)
    zeros_lora = jnp.zeros_like(rwkv_w_up)
    wup = jnp.concatenate([rwkv_w_up, zeros_lora], axis=1).astype(BF16)
    aup = jnp.concatenate([zeros_lora, rwkv_a_up], axis=1).astype(BF16)

    x2 = x.reshape(M, D)
    h, w_in_b, w_out_b = _norm_cast(x2, norm_w[0].reshape(1, D), w_in, w_out, tm_out)
    for l in range(depth):
        final = l == depth - 1
        l_next = None if final else l + 1
        p = _in_proj(h, w_in_b, w_in, l_next, tm_in, tn_in)
        if not final:
            p, w_in_b = p
        p3 = p.reshape(B, T, n_cols)
        ya = _lru(p3, lru_conv_w, lru_vec, lru_wa, lru_wx, l, lru_w, tt_lru)
        yb = _rwkv(p3, rwkv_par, mixwa, wup, aup, l, rwkv_col, rwkv_gate_col, rwkv_pairs, tt_rwkv)
        yc = _moba(p3, bown, bprev, moba_col, moba_gate_col, moba_pairs)
        nw = (final_norm_w if final else norm_w[l + 1]).reshape(1, D)
        res = _out_proj(x2, ya.reshape(M, lru_w), yb.reshape(M, rwkv_w), yc.reshape(M, moba_w),
                        w_out_b, nw, w_out, l_next, tm_out)
        if final:
            return res.reshape(B, T, D)
        x2, h, w_out_b = res
```

```python
import functools
import math

import numpy as np
import jax
import jax.numpy as jnp
from jax import lax
from jax.experimental import pallas as pl
from jax.experimental.pallas import tpu as pltpu

F32 = jnp.float32
BF16 = jnp.bfloat16

LANES = 128
HEAD_DIM = 64
NORM_EPS = 1e-6
LRU_CONV = 4
LRU_C = 8.0
RWKV_LORA = 64
RWKV_DECAY_SCALE = math.exp(-0.5)
RWKV_GN_EPS = 64e-5
RWKV_CHUNK = 64
MOBA_BLOCK = 256
MOBA_TOPK = 3
MOBA_VROWS = 80
REL_BUCKETS = 32
REL_MAX_DIST = 128
LOG2_E = math.log2(math.e)
VMEM_LIMIT = 60 * 1024 * 1024


def _dot(a, b, ca=1, cb=0):
    return lax.dot_general(a.astype(BF16), b.astype(BF16), (((ca,), (cb,)), ((), ())),
                           preferred_element_type=F32)


def _sigmoid(x):
    return 0.5 * jnp.tanh(0.5 * x) + 0.5


def _silu(x):
    return x * _sigmoid(x)


def _rms_norm(x, w):
    ms = jnp.mean(x * x, axis=-1, keepdims=True)
    return x * lax.rsqrt(ms + NORM_EPS) * w


def _norm_body(x_ref, nw_ref, wi_ref, wo_ref, h_ref, wib_ref, wob_ref):
    h_ref[...] = _rms_norm(x_ref[...], nw_ref[...]).astype(BF16)
    wib_ref[...] = wi_ref[...].astype(BF16)
    wob_ref[...] = wo_ref[...].astype(BF16)


def _slab_rows(D, steps):
    rows = D // steps
    assert rows * steps == D and rows % 16 == 0
    return rows


def _norm_cast(x2, nw, w_in, w_out, tm):
    M, D = x2.shape
    N = w_in.shape[2]
    rows = _slab_rows(D, M // tm)
    return pl.pallas_call(
        _norm_body,
        grid=(M // tm,),
        in_specs=[pl.BlockSpec((tm, D), lambda i: (i, 0)), pl.BlockSpec((1, D), lambda i: (0, 0)),
                  _layer(0, rows, N)(lambda i: (i, 0)), _layer(0, rows, D)(lambda i: (i, 0))],
        out_specs=[pl.BlockSpec((tm, D), lambda i: (i, 0)), pl.BlockSpec((rows, N), lambda i: (i, 0)),
                   pl.BlockSpec((rows, D), lambda i: (i, 0))],
        out_shape=[jax.ShapeDtypeStruct((M, D), BF16), jax.ShapeDtypeStruct((D, N), BF16),
                   jax.ShapeDtypeStruct((D, D), BF16)],
        compiler_params=pltpu.CompilerParams(dimension_semantics=("parallel",), vmem_limit_bytes=VMEM_LIMIT),
        name="norm_cast",
    )(x2, nw, w_in, w_out)


def _layer(l, *block):
    def spec(tail):
        return pl.BlockSpec((None,) + block, lambda *g: (l,) + tail(*g))
    return spec


def _inproj_body(h_ref, w_ref, *rest):
    if len(rest) == 3:
        wn_ref, o_ref, wnb_ref = rest
        wnb_ref[...] = wn_ref[...].astype(BF16)
    else:
        (o_ref,) = rest
    o_ref[...] = jnp.dot(h_ref[...], w_ref[...], preferred_element_type=F32)


def _in_proj(h, w, w_all, l_next, tm, tn):
    M, D = h.shape
    N = w.shape[1]
    in_specs = [pl.BlockSpec((tm, D), lambda j, i: (i, 0)), pl.BlockSpec((D, tn), lambda j, i: (0, j))]
    out_specs = pl.BlockSpec((tm, tn), lambda j, i: (i, j))
    out_shape = jax.ShapeDtypeStruct((M, N), F32)
    args = (h, w)
    if l_next is not None:
        rows = _slab_rows(D, M // tm)
        in_specs.append(_layer(l_next, rows, tn)(lambda j, i: (i, j)))
        out_specs = [out_specs, pl.BlockSpec((rows, tn), lambda j, i: (i, j))]
        out_shape = [out_shape, jax.ShapeDtypeStruct((D, N), BF16)]
        args += (w_all,)
    return pl.pallas_call(
        _inproj_body,
        grid=(N // tn, M // tm),
        in_specs=in_specs,
        out_specs=out_specs,
        out_shape=out_shape,
        compiler_params=pltpu.CompilerParams(
            dimension_semantics=("parallel", "parallel"), vmem_limit_bytes=VMEM_LIMIT),
        name="in_proj" if l_next is not None else "in_proj_last",
    )(*args)


def _outproj_body(x_ref, ya_ref, yb_ref, yc_ref, w_ref, nw_ref, *rest, final):
    wa, wb = ya_ref.shape[1], yb_ref.shape[1]
    acc = x_ref[...]
    acc = acc + _dot(ya_ref[...], w_ref[0:wa, :])
    acc = acc + _dot(yb_ref[...], w_ref[wa:wa + wb, :])
    acc = acc + _dot(yc_ref[...], w_ref[wa + wb:, :])
    normed = _rms_norm(acc, nw_ref[...])
    if final:
        rest[0][...] = normed
    else:
        wn_ref, x_out_ref, h_ref, wnb_ref = rest
        x_out_ref[...] = acc
        h_ref[...] = normed.astype(BF16)
        wnb_ref[...] = wn_ref[...].astype(BF16)


def _out_proj(x2, ya, yb, yc, w, nw, w_all, l_next, tm):
    M, D = x2.shape
    final = l_next is None
    row = lambda width: pl.BlockSpec((tm, width), lambda i: (i, 0))
    full = lambda a: pl.BlockSpec(a.shape, lambda i: (0, 0))
    in_specs = [row(D), row(ya.shape[1]), row(yb.shape[1]), row(yc.shape[1]), full(w), full(nw)]
    args = (x2, ya, yb, yc, w, nw)
    if final:
        out_specs, out_shape = row(D), jax.ShapeDtypeStruct((M, D), F32)
    else:
        rows = _slab_rows(D, M // tm)
        in_specs.append(_layer(l_next, rows, D)(lambda i: (i, 0)))
        args += (w_all,)
        out_specs = [row(D), row(D), pl.BlockSpec((rows, D), lambda i: (i, 0))]
        out_shape = [jax.ShapeDtypeStruct((M, D), F32), jax.ShapeDtypeStruct((M, D), BF16),
                     jax.ShapeDtypeStruct((D, D), BF16)]
    return pl.pallas_call(
        functools.partial(_outproj_body, final=final),
        grid=(M // tm,),
        in_specs=in_specs,
        out_specs=out_specs,
        out_shape=out_shape,
        compiler_params=pltpu.CompilerParams(
            dimension_semantics=("parallel",), vmem_limit_bytes=VMEM_LIMIT),
        name="out_proj_final" if final else "out_proj",
    )(*args)


def _lru_body(xa_ref, ga_ref, cw_ref, vec_ref, wa_ref, wx_ref, o_ref, xbuf, hcar):
    tt = xa_ref.shape[1]
    W = xa_ref.shape[2]

    @pl.when(pl.program_id(1) == 0)
    def _():
        xbuf[0:8, :] = jnp.zeros((8, W), F32)
        hcar[...] = jnp.zeros((8, W), F32)

    xa = xa_ref[0]
    xbuf[8:8 + tt, :] = xa
    xc = vec_ref[0:1, :] + cw_ref[LRU_CONV - 1:LRU_CONV, :] * xa
    for s in range(1, LRU_CONV):
        xc = xc + cw_ref[LRU_CONV - 1 - s:LRU_CONV - s, :] * xbuf[8 - s:8 - s + tt, :]
    xbuf[0:8, :] = xa[tt - 8:tt, :]

    r = _sigmoid(_dot(xc, wa_ref[...]) + vec_ref[1:2, :])
    gi = _sigmoid(_dot(xc, wx_ref[...]) + vec_ref[2:3, :])
    lam = vec_ref[3:4, :]
    softplus_neg_lam = jnp.maximum(-lam, 0.0) + jnp.log1p(jnp.exp(-jnp.abs(lam)))
    log_a = (-LRU_C) * r * softplus_neg_lam
    a = jnp.exp(log_a)
    one_minus_a2 = -jnp.tanh(log_a) * (a * a + 1.0)
    b = jnp.where(one_minus_a2 > 0.0, one_minus_a2 * lax.rsqrt(one_minus_a2), 0.0) * (gi * xc)

    row = lax.broadcasted_iota(jnp.int32, (tt, 1), 0)
    d = 1
    while d < tt:
        keep = row >= d
        a_sh = pltpu.roll(a, d, 0)
        b_sh = pltpu.roll(b, d, 0)
        b = b + jnp.where(keep, a * b_sh, 0.0)
        a = jnp.where(keep, a * a_sh, a)
        d *= 2
    h = b + a * hcar[0:1, :]
    hcar[...] = jnp.broadcast_to(h[tt - 1:tt, :], (8, W))
    o_ref[0] = (h * _silu(ga_ref[0])).astype(o_ref.dtype)


def _lru(p3, cw, vec, wa, wx, l, width, tt):
    B, T, _ = p3.shape
    full = lambda a: _layer(l, *a.shape[1:])(lambda b, t: (0, 0))
    return pl.pallas_call(
        _lru_body,
        grid=(B, T // tt),
        in_specs=[pl.BlockSpec((1, tt, width), lambda b, t: (b, t, 0)),
                  pl.BlockSpec((1, tt, width), lambda b, t: (b, t, 1)),
                  full(cw), full(vec), full(wa), full(wx)],
        out_specs=pl.BlockSpec((1, tt, width), lambda b, t: (b, t, 0)),
        out_shape=jax.ShapeDtypeStruct((B, T, width), BF16),
        scratch_shapes=[pltpu.VMEM((tt + 8, width), F32), pltpu.VMEM((8, width), F32)],
        compiler_params=pltpu.CompilerParams(
            dimension_semantics=("parallel", "arbitrary"), vmem_limit_bytes=VMEM_LIMIT),
        name="rg_lru",
    )(p3, p3, cw, vec, wa, wx)


def _rwkv_body(r_ref, k_ref, v_ref, wa_ref, g_ref, par_ref, mixwa_ref, wup_ref, aup_ref, o_ref,
               xs_ref, s_ref):
    C = RWKV_CHUNK
    NB, tt = r_ref.shape[0], r_ref.shape[1]
    nc = tt // C

    @pl.when(pl.program_id(1) == 0)
    def _():
        xs_ref[:, 0:8, :] = jnp.zeros((xs_ref.shape[0], 8, LANES), F32)
        s_ref[...] = jnp.zeros(s_ref.shape, F32)

    lane = lax.broadcasted_iota(jnp.int32, (1, LANES), 1)
    head0 = lane < HEAD_DIM
    rc = lax.broadcasted_iota(jnp.int32, (tt, 1), 0) & (C - 1)

    par = par_ref[...]
    w0, a0, k_k, k_a = par[0:1], par[1:2], par[2:3], par[3:4]
    ln_w, ln_b, r_k = par[4:5], par[5:6], par[6:7]
    mix_r, mix_k, mix_v = par[7:8], par[8:9], par[9:10]

    def shift_lerp(ref, b, mix, slot):
        s = ref[b]
        xs_ref[4 * b + slot, 8:8 + tt, :] = s
        prev = xs_ref[4 * b + slot, 7:7 + tt, :]
        xs_ref[4 * b + slot, 0:8, :] = s[tt - 8:tt, :]
        return s + mix * (prev - s)

    def headsum(x):
        s0 = jnp.sum(jnp.where(head0, x, 0.0), axis=-1, keepdims=True)
        s1 = jnp.sum(jnp.where(head0, 0.0, x), axis=-1, keepdims=True)
        return jnp.where(head0, s0, s1)

    def chunks(x):
        return [x[ch * C:(ch + 1) * C] for ch in range(nc)]

    def front(b):
        r = shift_lerp(r_ref, b, mix_r, 0)
        k = shift_lerp(k_ref, b, mix_k, 1)
        v = shift_lerp(v_ref, b, mix_v, 2)
        wa = shift_lerp(wa_ref, b, mixwa_ref[...], 3)
        log_w = (-RWKV_DECAY_SCALE) * _sigmoid(w0 + _dot(jnp.tanh(wa), wup_ref[...]))
        a = _sigmoid(a0 + _dot(wa, aup_ref[...]))
        kk = k * k_k
        kk = kk * lax.rsqrt(jnp.maximum(headsum(kk * kk), 1e-24))
        k = k * (1.0 + (a - 1.0) * k_a)
        be = kk * a
        c = log_w
        d = 1
        while d < C:
            c = c + jnp.where(rc >= d, pltpu.roll(c, d, 0), 0.0)
            d *= 2
        c_last = jnp.broadcast_to(c.reshape(nc, C, LANES)[:, C - 1:C, :], (nc, C, LANES)).reshape(tt, LANES)
        e_inc = jnp.exp(c)
        e_inv = jnp.exp(-c)
        e_end = jnp.exp(c_last - c)
        parts = dict(a=-kk * jnp.exp(c - log_w), r=r * e_inc, kt=k * e_inv, bt=be * e_inv,
                     kd=k * e_end, bd=be * e_end, v=v, gamma_end=jnp.exp(c_last))
        return {name: chunks(val) for name, val in parts.items()}, headsum(r * k * r_k) * v

    fronts = [front(b) for b in range(NB)]
    gather = lambda name: [x for f, _ in fronts for x in f[name]]

    ti = lax.broadcasted_iota(jnp.int32, (C, LANES), 0)
    si = lax.broadcasted_iota(jnp.int32, (C, LANES), 1) & (C - 1)
    strict = ti > si
    incl = ti >= si
    same16 = (ti >> 4) == (si >> 4)
    same32 = (ti >> 5) == (si >> 5)
    eye = (ti == si).astype(F32)
    same_head = ((lax.broadcasted_iota(jnp.int32, (LANES, LANES), 0) >> 6)
                 == (lax.broadcasted_iota(jnp.int32, (LANES, LANES), 1) >> 6))
    head0_b = jnp.broadcast_to(head0, (C, LANES))

    def embed(x):
        xb = x.astype(BF16)
        zero = jnp.zeros_like(xb)
        return jnp.concatenate([jnp.where(head0_b, xb, zero), jnp.where(head0_b, zero, xb)], axis=0)

    def each(f, *lists):
        return [f(*xs) for xs in zip(*lists)]

    lo = lambda x: x[:, 0:LANES]
    hi = lambda x: x[:, LANES:2 * LANES]
    side = lambda x, y: jnp.concatenate([x, y], axis=1)

    a_c = gather('a')
    r_c = gather('r')
    k_e = each(embed, gather('kt'))
    b_e = each(embed, gather('bt'))
    kd_c = gather('kd')
    bd_c = gather('bd')
    v_c = gather('v')
    v_e = each(embed, v_c)
    gamma_end = gather('gamma_end')

    p = each(lambda aa, rr, kt, bt: _dot(jnp.concatenate([aa, rr], axis=0),
                                         jnp.concatenate([kt, bt], axis=0), 1, 1), a_c, r_c, k_e, b_e)
    a_ak = each(lambda x: jnp.where(strict, x[0:C, 0:LANES], 0.0), p)
    n = each(lambda x: jnp.where(strict, x[0:C, LANES:2 * LANES], 0.0), p)
    a_rk = each(lambda x: jnp.where(incl, x[C:2 * C, 0:LANES], 0.0), p)
    a_rb = each(lambda x: jnp.where(incl, x[C:2 * C, LANES:2 * LANES], 0.0), p)
    q = each(lambda x: jnp.where(same16, x, 0.0), n)
    t = each(lambda x: eye + x, q)
    q = each(lambda x: _dot(x, embed(x)), q)
    for _ in range(2):
        qt = each(lambda x, y: _dot(x, side(embed(y), embed(x))), q, t)
        t = each(lambda x, y: x + lo(y), t, qt)
        q = each(hi, qt)
    t = each(lambda x, y: x + _dot(y, embed(x)), t, q)
    tn = each(lambda x, y: _dot(x, embed(jnp.where(same32 & jnp.logical_not(same16), y, 0.0))), t, n)
    t = each(lambda x, y: x + _dot(y, embed(x)), t, tn)
    tn = each(lambda x, y: _dot(x, embed(jnp.where(same32, 0.0, y))), t, n)
    t = each(lambda x, y: x + _dot(y, embed(x)), t, tn)
    av = each(lambda x, y, vv: _dot(jnp.concatenate([x, y], axis=0), vv), a_ak, a_rk, v_e)
    taw = each(lambda x, aa, w: _dot(x, side(embed(aa), embed(w[0:C]))), t, a_c, av)
    ry = each(lambda x, y: _dot(x, side(embed(lo(y)), embed(hi(y)))), a_rb, taw)
    rq = each(lambda rr, x: rr + lo(x), r_c, ry)
    y0 = each(lambda w, x: w[C:2 * C] + hi(x), av, ry)
    g = each(lambda x, bd: jnp.where(same_head, _dot(lo(x), bd, 0, 0), 0.0), taw, bd_c)
    s0 = each(lambda vv, x, kd, bd: jnp.where(same_head, _dot(jnp.concatenate([vv, hi(x)], axis=0),
                                                              jnp.concatenate([kd, bd], axis=0), 0, 0), 0.0),
              v_c, taw, kd_c, bd_c)

    states = [s_ref[b] for b in range(NB)]
    ys = [[] for _ in range(NB)]
    for ch in range(nc):
        for b in range(NB):
            i, s = b * nc + ch, states[b]
            ys[b].append(_dot(rq[i], s, 1, 1) + y0[i])
            states[b] = s * gamma_end[i][C - 1:C, :] + _dot(s, g[i]) + s0[i]
    for b in range(NB):
        s_ref[b] = states[b]
        y = jnp.concatenate(ys[b], axis=0)
        mu = headsum(y) * (1.0 / HEAD_DIM)
        yc = y - mu
        var = headsum(yc * yc) * (1.0 / HEAD_DIM)
        yn = yc * lax.rsqrt(var + RWKV_GN_EPS) * ln_w + ln_b
        o_ref[b] = ((yn + fronts[b][1]) * _silu(g_ref[b])).astype(o_ref.dtype)


def _rwkv(p3, par, mixwa, wup, aup, l, col0, gcol0, npairs, tt):
    B, T, _ = p3.shape
    col = lambda off: pl.BlockSpec((B, tt, LANES), lambda j, t: (0, t, off + j))
    fixed = lambda off: pl.BlockSpec((B, tt, LANES), lambda j, t: (0, t, off))
    return pl.pallas_call(
        _rwkv_body,
        grid=(npairs, T // tt),
        in_specs=[col(col0), col(col0 + npairs), col(col0 + 2 * npairs), fixed(col0 + 3 * npairs),
                  col(gcol0),
                  _layer(l, 16, LANES)(lambda j, t: (0, j)),
                  _layer(l, 1, LANES)(lambda j, t: (0, 0)),
                  _layer(l, LANES, LANES)(lambda j, t: (0, j)),
                  _layer(l, LANES, LANES)(lambda j, t: (0, j))],
        out_specs=pl.BlockSpec((B, tt, LANES), lambda j, t: (0, t, j)),
        out_shape=jax.ShapeDtypeStruct((B, T, npairs * LANES), BF16),
        scratch_shapes=[pltpu.VMEM((4 * B, tt + 8, LANES), F32), pltpu.VMEM((B, LANES, LANES), F32)],
        compiler_params=pltpu.CompilerParams(
            dimension_semantics=("parallel", "arbitrary"), vmem_limit_bytes=VMEM_LIMIT),
        name="rwkv7",
    )(p3, p3, p3, p3, p3, par, mixwa, wup, aup)


def _moba_body(q_ref, k_ref, v_ref, g_ref, bown_ref, bprev_ref, o_ref, kmean_ref, kb_ref, vt_ref, acc_ref,
               sc_ref, p_ref):
    BS = MOBA_BLOCK
    n_batch, T = k_ref.shape[0], kb_ref.shape[1]
    NB = n_batch * (k_ref.shape[2] // LANES)
    nb = T // BS
    i = pl.program_id(1)
    lane = lax.broadcasted_iota(jnp.int32, (1, LANES), 1)
    head0 = lane < HEAD_DIM
    streams = [(b, h) for b in range(NB) for h in range(2)]

    def tile(ref, u, rows=slice(None)):
        return ref.at[u % n_batch, rows, (u // n_batch) * LANES:(u // n_batch + 1) * LANES]

    def table(ref, bh):
        return ref[2 * (bh[0] // n_batch) + bh[1]]

    def each(f, *lists):
        return [f(*xs) for xs in zip(*lists)]

    VR = MOBA_VROWS

    @pl.when(i == 0)
    def _():
        kmean_ref[...] = jnp.zeros(kmean_ref.shape, F32)

    own_rows = pl.ds(pl.multiple_of(i * BS, BS), BS)
    ones = jnp.ones((VR - HEAD_DIM, BS), BF16)
    for b in range(NB):
        k_blk = tile(k_ref, b)[...]
        kmean_ref[b, pl.ds(i, 1), :] = jnp.mean(k_blk, axis=0, keepdims=True)
        kb_ref[b, own_rows, :] = k_blk.astype(BF16)
        vt = tile(v_ref, b)[...].T.astype(BF16)
        vt_ref[b, i] = jnp.concatenate([vt[0:HEAD_DIM], ones, vt[HEAD_DIM:2 * HEAD_DIM], ones], axis=0)

    qh = []
    for b in range(NB):
        q = tile(q_ref, b)[...]
        qh += [jnp.where(head0, q, 0.0), jnp.where(head0, 0.0, q)]
    q_scaled = [x * (HEAD_DIM ** -0.5 * LOG2_E) for x in qh]
    qs = [x.astype(BF16) for x in q_scaled]

    def scores(j):
        rows = pl.ds(pl.multiple_of(j * BS, BS), BS)
        return [_dot(kb_ref[b, rows, :], qs[s], 1, 1) for s, (b, _) in enumerate(streams)]

    def weighted(j, s, p):
        b, h = streams[s]
        return _dot(vt_ref[b, j, h * VR:(h + 1) * VR, :], p)

    def acc_at(s):
        b, h = streams[s]
        return b, slice(h * VR, (h + 1) * VR)

    def split(x):
        hi = x.astype(BF16)
        return hi, (x - hi.astype(F32)).astype(BF16)

    prev_blk = jnp.maximum(i - 1, 0)
    kmean_parts = [split(kmean_ref[b]) for b in range(NB)]
    own = [_dot(jnp.concatenate([kb_ref[b, own_rows, :], kmean_parts[b][0], kmean_parts[b][1]], axis=0), qs[s], 1, 1)
           for s, (b, _) in enumerate(streams)]
    s_own = each(lambda x, bh: x[0:BS] + table(bown_ref, bh), own, streams)
    gate = [x[BS:BS + nb] + x[BS + nb:BS + 2 * nb]
            + _dot(kmean_parts[b][0], (xq - q.astype(F32)).astype(BF16), 1, 1)
            for x, xq, q, (b, _) in zip(own, q_scaled, qs, streams)]
    s_prev = each(lambda x, bh: x + table(bprev_ref, bh), scores(prev_blk), streams)
    for s, x in enumerate(scores(0)):
        sc_ref[s] = x

    m = each(lambda x: jnp.max(x, axis=0, keepdims=True), s_own)
    for s in range(len(streams)):
        b, rows = acc_at(s)
        acc_ref[b, rows, :] = weighted(i, s, jnp.exp2(s_own[s] - m[s]))

    blk = lax.broadcasted_iota(jnp.int32, (nb, 1), 0).astype(F32)
    i_f = i.astype(F32)
    gate = each(lambda x: jnp.where(blk < i_f, x, -jnp.inf), gate)
    sel = []
    for rnk in range(MOBA_TOPK):
        top = each(lambda x: jnp.max(x, axis=0, keepdims=True), gate)
        idx = each(lambda x, tp: jnp.min(jnp.where(x == tp, blk, float(nb)), axis=0, keepdims=True), gate, top)
        gate = each(lambda x, ix: jnp.where(blk == ix, -jnp.inf, x), gate, idx)
        sel.append(each(lambda ix: jnp.where(rnk < i, ix, -5.0), idx))

    def chosen(s, j):
        j_f = j.astype(F32)
        return (sel[0][s] == j_f) | (sel[1][s] == j_f) | (sel[2][s] == j_f)

    def softmax_step(j, sc, m, valid):
        m_new, alpha = [], []
        for s in range(len(streams)):
            take = chosen(s, j) & valid
            top = jnp.max(sc[s], axis=0, keepdims=True)
            mn = jnp.maximum(m[s], jnp.where(take, top, -jnp.inf))
            p_ref[s] = jnp.exp2(sc[s] - jnp.where(take, mn, jnp.inf)).astype(BF16)
            m_new.append(mn)
            alpha.append(jnp.exp2(m[s] - mn))
        return m_new, alpha

    def accumulate(j, alpha):
        for s in range(len(streams)):
            b, rows = acc_at(s)
            acc_ref[b, rows, :] = alpha[s] * acc_ref[b, rows, :] + weighted(j, s, p_ref[s])

    m, alpha = softmax_step(prev_blk, s_prev, m, i >= 1)

    def far_block(j, st):
        m, alpha = list(st[0]), list(st[1])
        nxt = scores(jnp.minimum(j + 1, nb - 1))
        accumulate(jnp.where(j == 0, prev_blk, j - 1), alpha)
        m, alpha = softmax_step(j, [sc_ref[s] for s in range(len(streams))], m, True)
        for s, x in enumerate(nxt):
            sc_ref[s] = x
        return tuple(m), tuple(alpha)

    n_far = jnp.maximum(i - 1, 0)
    m, alpha = lax.fori_loop(0, n_far, far_block, (tuple(m), tuple(alpha)))
    accumulate(jnp.where(n_far == 0, prev_blk, n_far - 1), alpha)

    for b in range(NB):
        acc = acc_ref[b]
        out_t = jnp.concatenate([acc[h * VR:h * VR + HEAD_DIM] * (1.0 / acc[h * VR + HEAD_DIM:h * VR + HEAD_DIM + 1])
                                 for h in range(2)], axis=0)
        tile(o_ref, b)[...] = (out_t.T * _silu(tile(g_ref, b)[...])).astype(o_ref.dtype)


def _moba(p3, bown, bprev, col0, gcol0, npairs, tiles):
    B, T, _ = p3.shape
    BS = MOBA_BLOCK
    nb = T // BS
    W = tiles * LANES
    assert npairs % tiles == 0 and col0 % tiles == 0 and gcol0 % tiles == 0
    groups = npairs // tiles
    nseq = tiles * B
    return pl.pallas_call(
        _moba_body,
        grid=(groups, nb),
        in_specs=[pl.BlockSpec((B, BS, W), lambda j, i: (0, i, col0 // tiles + j)),
                  pl.BlockSpec((B, BS, W), lambda j, i: (0, i, col0 // tiles + groups + j)),
                  pl.BlockSpec((B, BS, W), lambda j, i: (0, i, col0 // tiles + 2 * groups + j)),
                  pl.BlockSpec((B, BS, W), lambda j, i: (0, i, gcol0 // tiles + j)),
                  pl.BlockSpec((2 * tiles, BS, BS), lambda j, i: (j, 0, 0)),
                  pl.BlockSpec((2 * tiles, BS, BS), lambda j, i: (j, 0, 0))],
        out_specs=pl.BlockSpec((B, BS, W), lambda j, i: (0, i, j)),
        out_shape=jax.ShapeDtypeStruct((B, T, npairs * LANES), BF16),
        scratch_shapes=[pltpu.VMEM((nseq, nb, LANES), F32), pltpu.VMEM((nseq, T, LANES), BF16),
                        pltpu.VMEM((nseq, nb, 2 * MOBA_VROWS, BS), BF16), pltpu.VMEM((nseq, 2 * MOBA_VROWS, BS), F32),
                        pltpu.VMEM((2 * nseq, BS, BS), F32), pltpu.VMEM((2 * nseq, BS, BS), BF16)],
        compiler_params=pltpu.CompilerParams(
            dimension_semantics=("parallel", "arbitrary"), vmem_limit_bytes=VMEM_LIMIT),
        name="moba",
    )(p3, p3, p3, p3, bown, bprev)


def _rel_bucket_of(dist):
    max_exact = REL_BUCKETS // 2
    large = max_exact + (jnp.log(jnp.maximum(dist, 1).astype(F32) / max_exact)
                         / math.log(REL_MAX_DIST / max_exact) * (REL_BUCKETS - max_exact)).astype(jnp.int32)
    large = jnp.minimum(large, REL_BUCKETS - 1)
    return jnp.where(dist < max_exact, dist, large)


def _toeplitz_kq(tab):
    H, L = tab.shape
    BS = L // 2
    flat = jnp.tile(jnp.roll(tab, -1, axis=1), (1, BS))[:, :BS * (L - 1)]
    return flat.reshape(H, BS, L - 1)[:, :, BS - 1:2 * BS - 1]


def _moba_bias_tables(rel_bias):
    BS = MOBA_BLOCK
    per_dist = rel_bias.astype(F32)[_rel_bucket_of(jnp.arange(2 * BS))].T
    per_dist = (per_dist - per_dist[:, 2 * BS - 1:2 * BS]) * LOG2_E
    prev = _toeplitz_kq(per_dist)
    ki = np.arange(BS)[:, None]
    qi = np.arange(BS)[None, :]
    own = jnp.where(ki <= qi, _toeplitz_kq(jnp.roll(per_dist, BS, axis=1)), -jnp.inf)
    return own, prev


def _block_diag(w):
    d, g, n, _ = w.shape
    return jnp.einsum('lgij,gh->lgihj', w, jnp.eye(g, dtype=w.dtype)).reshape(d, g * n, g * n)


def kernel(x, norm_w, w_in, w_out, lru_conv_w, lru_conv_b, lru_gate_a_w, lru_gate_a_b, lru_gate_x_w, lru_gate_x_b, lru_lambda, rwkv_mix, rwkv_w0, rwkv_w_up, rwkv_a0, rwkv_a_up, rwkv_k_k, rwkv_k_a, rwkv_r_k, rwkv_ln_w, rwkv_ln_b, rel_bias, final_norm_w):
    B, T, D = x.shape
    depth = w_in.shape[0]
    lru_w = lru_conv_w.shape[2]
    rwkv_w = rwkv_w0.shape[1]
    moba_w = rel_bias.shape[1] * HEAD_DIM
    rwkv_pairs = rwkv_w // LANES
    moba_pairs = moba_w // LANES
    rwkv_col = (2 * lru_w) // LANES
    rwkv_gate_col = rwkv_col + (3 * rwkv_w + 2 * RWKV_LORA) // LANES
    moba_col = rwkv_gate_col + rwkv_w // LANES
    moba_gate_col = moba_col + (3 * moba_w) // LANES
    n_cols = w_in.shape[2]
    assert (moba_gate_col + moba_w // LANES) * LANES == n_cols
    assert T % MOBA_BLOCK == 0 and 2 * RWKV_LORA == LANES

    M = B * T
    tm_in = 1024
    tn_in = n_cols // 3
    tm_out = 512
    tt_lru = 512
    tt_rwkv = 1024
    moba_tiles = 3

    bown, bprev = _moba_bias_tables(rel_bias)
    lru_vec = jnp.stack([lru_conv_b, lru_gate_a_b, lru_gate_x_b, lru_lambda] + [jnp.zeros_like(lru_lambda)] * 4,
                        axis=1)
    lru_wa = _block_diag(lru_gate_a_w).astype(BF16)
    lru_wx = _block_diag(lru_gate_x_w).astype(BF16)
    mix = rwkv_mix
    rwkv_par = jnp.stack([rwkv_w0, rwkv_a0, rwkv_k_k, rwkv_k_a, rwkv_ln_w, rwkv_ln_b,
                          rwkv_r_k.reshape(depth, rwkv_w), mix[:, 0:rwkv_w], mix[:, rwkv_w:2 * rwkv_w],
                          mix[:, 2 * rwkv_w:3 * rwkv_w]] + [jnp.zeros_like(rwkv_w0)] * 6, axis=1)
    mixwa = mix[:, 3 * rwkv_w:].reshape(depth, 1, LANES)
    zeros_lora = jnp.zeros_like(rwkv_w_up)
    wup = jnp.concatenate([rwkv_w_up, zeros_lora], axis=1).astype(BF16)
    aup = jnp.concatenate([zeros_lora, rwkv_a_up], axis=1).astype(BF16)

    x2 = x.reshape(M, D)
    h, w_in_b, w_out_b = _norm_cast(x2, norm_w[0].reshape(1, D), w_in, w_out, tm_out)
    for l in range(depth):
        final = l == depth - 1
        l_next = None if final else l + 1
        p = _in_proj(h, w_in_b, w_in, l_next, tm_in, tn_in)
        if not final:
            p, w_in_b = p
        p3 = p.reshape(B, T, n_cols)
        ya = _lru(p3, lru_conv_w, lru_vec, lru_wa, lru_wx, l, lru_w, tt_lru)
        yb = _rwkv(p3, rwkv_par, mixwa, wup, aup, l, rwkv_col, rwkv_gate_col, rwkv_pairs, tt_rwkv)
        yc = _moba(p3, bown, bprev, moba_col, moba_gate_col, moba_pairs, moba_tiles)
        nw = (final_norm_w if final else norm_w[l + 1]).reshape(1, D)
        res = _out_proj(x2, ya.reshape(M, lru_w), yb.reshape(M, rwkv_w), yc.reshape(M, moba_w),
                        w_out_b, nw, w_out, l_next, tm_out)
        if final:
            return res.reshape(B, T, D)
        x2, h, w_out_b = res
```

```python
import functools
import math

import numpy as np
import jax
import jax.numpy as jnp
from jax import lax
from jax.experimental import pallas as pl
from jax.experimental.pallas import tpu as pltpu

F32 = jnp.float32
BF16 = jnp.bfloat16

LANES = 128
HEAD_DIM = 64
NORM_EPS = 1e-6
LRU_CONV = 4
LRU_C = 8.0
RWKV_LORA = 64
RWKV_DECAY_SCALE = math.exp(-0.5)
RWKV_GN_EPS = 64e-5
RWKV_CHUNK = 64
MOBA_BLOCK = 256
MOBA_TOPK = 3
MOBA_VROWS = 80
REL_BUCKETS = 32
REL_MAX_DIST = 128
LOG2_E = math.log2(math.e)
VMEM_LIMIT = 60 * 1024 * 1024


def _dot(a, b, ca=1, cb=0):
    return lax.dot_general(a.astype(BF16), b.astype(BF16), (((ca,), (cb,)), ((), ())),
                           preferred_element_type=F32)


def _sigmoid(x):
    return 0.5 * jnp.tanh(0.5 * x) + 0.5


def _silu(x):
    return x * _sigmoid(x)


def _rms_norm(x, w):
    ms = jnp.mean(x * x, axis=-1, keepdims=True)
    return x * lax.rsqrt(ms + NORM_EPS) * w


def _norm_body(x_ref, nw_ref, wi_ref, wo_ref, h_ref, wib_ref, wob_ref):
    h_ref[...] = _rms_norm(x_ref[...], nw_ref[...]).astype(BF16)
    wib_ref[...] = wi_ref[...].astype(BF16)
    wob_ref[...] = wo_ref[...].astype(BF16)


def _slab_rows(D, steps):
    rows = D // steps
    assert rows * steps == D and rows % 16 == 0
    return rows


def _norm_cast(x2, nw, w_in, w_out, tm):
    M, D = x2.shape
    N = w_in.shape[2]
    rows = _slab_rows(D, M // tm)
    return pl.pallas_call(
        _norm_body,
        grid=(M // tm,),
        in_specs=[pl.BlockSpec((tm, D), lambda i: (i, 0)), pl.BlockSpec((1, D), lambda i: (0, 0)),
                  _layer(0, rows, N)(lambda i: (i, 0)), _layer(0, rows, D)(lambda i: (i, 0))],
        out_specs=[pl.BlockSpec((tm, D), lambda i: (i, 0)), pl.BlockSpec((rows, N), lambda i: (i, 0)),
                   pl.BlockSpec((rows, D), lambda i: (i, 0))],
        out_shape=[jax.ShapeDtypeStruct((M, D), BF16), jax.ShapeDtypeStruct((D, N), BF16),
                   jax.ShapeDtypeStruct((D, D), BF16)],
        compiler_params=pltpu.CompilerParams(dimension_semantics=("parallel",), vmem_limit_bytes=VMEM_LIMIT),
        name="norm_cast",
    )(x2, nw, w_in, w_out)


def _layer(l, *block):
    def spec(tail):
        return pl.BlockSpec((None,) + block, lambda *g: (l,) + tail(*g))
    return spec


def _inproj_body(h_ref, w_ref, *rest):
    if len(rest) == 3:
        wn_ref, o_ref, wnb_ref = rest
        wnb_ref[...] = wn_ref[...].astype(BF16)
    else:
        (o_ref,) = rest
    o_ref[...] = jnp.dot(h_ref[...], w_ref[...], preferred_element_type=F32)


def _in_proj(h, w, w_all, l_next, tm, tn):
    M, D = h.shape
    N = w.shape[1]
    in_specs = [pl.BlockSpec((tm, D), lambda j, i: (i, 0)), pl.BlockSpec((D, tn), lambda j, i: (0, j))]
    out_specs = pl.BlockSpec((tm, tn), lambda j, i: (i, j))
    out_shape = jax.ShapeDtypeStruct((M, N), F32)
    args = (h, w)
    if l_next is not None:
        rows = _slab_rows(D, M // tm)
        in_specs.append(_layer(l_next, rows, tn)(lambda j, i: (i, j)))
        out_specs = [out_specs, pl.BlockSpec((rows, tn), lambda j, i: (i, j))]
        out_shape = [out_shape, jax.ShapeDtypeStruct((D, N), BF16)]
        args += (w_all,)
    return pl.pallas_call(
        _inproj_body,
        grid=(N // tn, M // tm),
        in_specs=in_specs,
        out_specs=out_specs,
        out_shape=out_shape,
        compiler_params=pltpu.CompilerParams(
            dimension_semantics=("parallel", "parallel"), vmem_limit_bytes=VMEM_LIMIT),
        name="in_proj" if l_next is not None else "in_proj_last",
    )(*args)


def _outproj_body(x_ref, ya_ref, yb_ref, yc_ref, w_ref, nw_ref, *rest, final):
    wa, wb = ya_ref.shape[1], yb_ref.shape[1]
    acc = x_ref[...]
    acc = acc + _dot(ya_ref[...], w_ref[0:wa, :])
    acc = acc + _dot(yb_ref[...], w_ref[wa:wa + wb, :])
    acc = acc + _dot(yc_ref[...], w_ref[wa + wb:, :])
    normed = _rms_norm(acc, nw_ref[...])
    if final:
        rest[0][...] = normed
    else:
        wn_ref, x_out_ref, h_ref, wnb_ref = rest
        x_out_ref[...] = acc
        h_ref[...] = normed.astype(BF16)
        wnb_ref[...] = wn_ref[...].astype(BF16)


def _out_proj(x2, ya, yb, yc, w, nw, w_all, l_next, tm):
    M, D = x2.shape
    final = l_next is None
    row = lambda width: pl.BlockSpec((tm, width), lambda i: (i, 0))
    full = lambda a: pl.BlockSpec(a.shape, lambda i: (0, 0))
    in_specs = [row(D), row(ya.shape[1]), row(yb.shape[1]), row(yc.shape[1]), full(w), full(nw)]
    args = (x2, ya, yb, yc, w, nw)
    if final:
        out_specs, out_shape = row(D), jax.ShapeDtypeStruct((M, D), F32)
    else:
        rows = _slab_rows(D, M // tm)
        in_specs.append(_layer(l_next, rows, D)(lambda i: (i, 0)))
        args += (w_all,)
        out_specs = [row(D), row(D), pl.BlockSpec((rows, D), lambda i: (i, 0))]
        out_shape = [jax.ShapeDtypeStruct((M, D), F32), jax.ShapeDtypeStruct((M, D), BF16),
                     jax.ShapeDtypeStruct((D, D), BF16)]
    return pl.pallas_call(
        functools.partial(_outproj_body, final=final),
        grid=(M // tm,),
        in_specs=in_specs,
        out_specs=out_specs,
        out_shape=out_shape,
        compiler_params=pltpu.CompilerParams(
            dimension_semantics=("parallel",), vmem_limit_bytes=VMEM_LIMIT),
        name="out_proj_final" if final else "out_proj",
    )(*args)


def _lru_body(xa_ref, ga_ref, cw_ref, vec_ref, wa_ref, wx_ref, o_ref, xbuf, hcar):
    tt = xa_ref.shape[1]
    W = xa_ref.shape[2]

    @pl.when(pl.program_id(1) == 0)
    def _():
        xbuf[0:8, :] = jnp.zeros((8, W), F32)
        hcar[...] = jnp.zeros((8, W), F32)

    xa = xa_ref[0]
    xbuf[8:8 + tt, :] = xa
    xc = vec_ref[0:1, :] + cw_ref[LRU_CONV - 1:LRU_CONV, :] * xa
    for s in range(1, LRU_CONV):
        xc = xc + cw_ref[LRU_CONV - 1 - s:LRU_CONV - s, :] * xbuf[8 - s:8 - s + tt, :]
    xbuf[0:8, :] = xa[tt - 8:tt, :]

    r = _sigmoid(_dot(xc, wa_ref[...]) + vec_ref[1:2, :])
    gi = _sigmoid(_dot(xc, wx_ref[...]) + vec_ref[2:3, :])
    lam = vec_ref[3:4, :]
    softplus_neg_lam = jnp.maximum(-lam, 0.0) + jnp.log1p(jnp.exp(-jnp.abs(lam)))
    log_a = (-LRU_C) * r * softplus_neg_lam
    a = jnp.exp(log_a)
    one_minus_a2 = -jnp.tanh(log_a) * (a * a + 1.0)
    b = jnp.where(one_minus_a2 > 0.0, one_minus_a2 * lax.rsqrt(one_minus_a2), 0.0) * (gi * xc)

    row = lax.broadcasted_iota(jnp.int32, (tt, 1), 0)
    d = 1
    while d < tt:
        keep = row >= d
        a_sh = pltpu.roll(a, d, 0)
        b_sh = pltpu.roll(b, d, 0)
        b = b + jnp.where(keep, a * b_sh, 0.0)
        a = jnp.where(keep, a * a_sh, a)
        d *= 2
    h = b + a * hcar[0:1, :]
    hcar[...] = jnp.broadcast_to(h[tt - 1:tt, :], (8, W))
    o_ref[0] = (h * _silu(ga_ref[0])).astype(o_ref.dtype)


def _lru(p3, cw, vec, wa, wx, l, width, tt):
    B, T, _ = p3.shape
    full = lambda a: _layer(l, *a.shape[1:])(lambda b, t: (0, 0))
    return pl.pallas_call(
        _lru_body,
        grid=(B, T // tt),
        in_specs=[pl.BlockSpec((1, tt, width), lambda b, t: (b, t, 0)),
                  pl.BlockSpec((1, tt, width), lambda b, t: (b, t, 1)),
                  full(cw), full(vec), full(wa), full(wx)],
        out_specs=pl.BlockSpec((1, tt, width), lambda b, t: (b, t, 0)),
        out_shape=jax.ShapeDtypeStruct((B, T, width), BF16),
        scratch_shapes=[pltpu.VMEM((tt + 8, width), F32), pltpu.VMEM((8, width), F32)],
        compiler_params=pltpu.CompilerParams(
            dimension_semantics=("parallel", "arbitrary"), vmem_limit_bytes=VMEM_LIMIT),
        name="rg_lru",
    )(p3, p3, cw, vec, wa, wx)


def _rwkv_body(r_ref, k_ref, v_ref, wa_ref, g_ref, par_ref, mixwa_ref, wup_ref, aup_ref, o_ref,
               xs_ref, s_ref):
    C = RWKV_CHUNK
    NB, tt = r_ref.shape[0], r_ref.shape[1]
    nc = tt // C

    @pl.when(pl.program_id(1) == 0)
    def _():
        xs_ref[:, 0:8, :] = jnp.zeros((xs_ref.shape[0], 8, LANES), F32)
        s_ref[...] = jnp.zeros(s_ref.shape, F32)

    lane = lax.broadcasted_iota(jnp.int32, (1, LANES), 1)
    head0 = lane < HEAD_DIM
    rc = lax.broadcasted_iota(jnp.int32, (tt, 1), 0) & (C - 1)

    par = par_ref[...]
    w0, a0, k_k, k_a = par[0:1], par[1:2], par[2:3], par[3:4]
    ln_w, ln_b, r_k = par[4:5], par[5:6], par[6:7]
    mix_r, mix_k, mix_v = par[7:8], par[8:9], par[9:10]

    def shift_lerp(ref, b, mix, slot):
        s = ref[b]
        xs_ref[4 * b + slot, 8:8 + tt, :] = s
        prev = xs_ref[4 * b + slot, 7:7 + tt, :]
        xs_ref[4 * b + slot, 0:8, :] = s[tt - 8:tt, :]
        return s + mix * (prev - s)

    def headsum(x):
        s0 = jnp.sum(jnp.where(head0, x, 0.0), axis=-1, keepdims=True)
        s1 = jnp.sum(jnp.where(head0, 0.0, x), axis=-1, keepdims=True)
        return jnp.where(head0, s0, s1)

    def chunks(x):
        return [x[ch * C:(ch + 1) * C] for ch in range(nc)]

    def front(b):
        r = shift_lerp(r_ref, b, mix_r, 0)
        k = shift_lerp(k_ref, b, mix_k, 1)
        v = shift_lerp(v_ref, b, mix_v, 2)
        wa = shift_lerp(wa_ref, b, mixwa_ref[...], 3)
        log_w = (-RWKV_DECAY_SCALE) * _sigmoid(w0 + _dot(jnp.tanh(wa), wup_ref[...]))
        a = _sigmoid(a0 + _dot(wa, aup_ref[...]))
        kk = k * k_k
        kk = kk * lax.rsqrt(jnp.maximum(headsum(kk * kk), 1e-24))
        k = k * (1.0 + (a - 1.0) * k_a)
        be = kk * a
        c = log_w
        d = 1
        while d < C:
            c = c + jnp.where(rc >= d, pltpu.roll(c, d, 0), 0.0)
            d *= 2
        c_last = jnp.broadcast_to(c.reshape(nc, C, LANES)[:, C - 1:C, :], (nc, C, LANES)).reshape(tt, LANES)
        e_inc = jnp.exp(c)
        e_inv = jnp.exp(-c)
        e_end = jnp.exp(c_last - c)
        parts = dict(a=-kk * jnp.exp(c - log_w), r=r * e_inc, kt=k * e_inv, bt=be * e_inv,
                     kd=k * e_end, bd=be * e_end, v=v, gamma_end=jnp.exp(c_last))
        return {name: chunks(val) for name, val in parts.items()}, headsum(r * k * r_k) * v

    fronts = [front(b) for b in range(NB)]
    gather = lambda name: [x for f, _ in fronts for x in f[name]]

    ti = lax.broadcasted_iota(jnp.int32, (C, LANES), 0)
    si = lax.broadcasted_iota(jnp.int32, (C, LANES), 1) & (C - 1)
    strict = ti > si
    incl = ti >= si
    same16 = (ti >> 4) == (si >> 4)
    same32 = (ti >> 5) == (si >> 5)
    eye = (ti == si).astype(F32)
    same_head = ((lax.broadcasted_iota(jnp.int32, (LANES, LANES), 0) >> 6)
                 == (lax.broadcasted_iota(jnp.int32, (LANES, LANES), 1) >> 6))
    head0_b = jnp.broadcast_to(head0, (C, LANES))

    def embed(x):
        xb = x.astype(BF16)
        zero = jnp.zeros_like(xb)
        return jnp.concatenate([jnp.where(head0_b, xb, zero), jnp.where(head0_b, zero, xb)], axis=0)

    def each(f, *lists):
        return [f(*xs) for xs in zip(*lists)]

    lo = lambda x: x[:, 0:LANES]
    hi = lambda x: x[:, LANES:2 * LANES]
    side = lambda x, y: jnp.concatenate([x, y], axis=1)

    a_c = gather('a')
    r_c = gather('r')
    k_e = each(embed, gather('kt'))
    b_e = each(embed, gather('bt'))
    kd_c = gather('kd')
    bd_c = gather('bd')
    v_c = gather('v')
    v_e = each(embed, v_c)
    gamma_end = gather('gamma_end')

    p = each(lambda aa, rr, kt, bt: _dot(jnp.concatenate([aa, rr], axis=0),
                                         jnp.concatenate([kt, bt], axis=0), 1, 1), a_c, r_c, k_e, b_e)
    a_ak = each(lambda x: jnp.where(strict, x[0:C, 0:LANES], 0.0), p)
    n = each(lambda x: jnp.where(strict, x[0:C, LANES:2 * LANES], 0.0), p)
    a_rk = each(lambda x: jnp.where(incl, x[C:2 * C, 0:LANES], 0.0), p)
    a_rb = each(lambda x: jnp.where(incl, x[C:2 * C, LANES:2 * LANES], 0.0), p)
    q = each(lambda x: jnp.where(same16, x, 0.0), n)
    t = each(lambda x: eye + x, q)
    q = each(lambda x: _dot(x, embed(x)), q)
    for _ in range(2):
        qt = each(lambda x, y: _dot(x, side(embed(y), embed(x))), q, t)
        t = each(lambda x, y: x + lo(y), t, qt)
        q = each(hi, qt)
    t = each(lambda x, y: x + _dot(y, embed(x)), t, q)
    tn = each(lambda x, y: _dot(x, embed(jnp.where(same32 & jnp.logical_not(same16), y, 0.0))), t, n)
    t = each(lambda x, y: x + _dot(y, embed(x)), t, tn)
    tn = each(lambda x, y: _dot(x, embed(jnp.where(same32, 0.0, y))), t, n)
    t = each(lambda x, y: x + _dot(y, embed(x)), t, tn)
    av = each(lambda x, y, vv: _dot(jnp.concatenate([x, y], axis=0), vv), a_ak, a_rk, v_e)
    taw = each(lambda x, aa, w: _dot(x, side(embed(aa), embed(w[0:C]))), t, a_c, av)
    ry = each(lambda x, y: _dot(x, side(embed(lo(y)), embed(hi(y)))), a_rb, taw)
    rq = each(lambda rr, x: rr + lo(x), r_c, ry)
    y0 = each(lambda w, x: w[C:2 * C] + hi(x), av, ry)
    g = each(lambda x, bd: jnp.where(same_head, _dot(lo(x), bd, 0, 0), 0.0), taw, bd_c)
    s0 = each(lambda vv, x, kd, bd: jnp.where(same_head, _dot(jnp.concatenate([vv, hi(x)], axis=0),
                                                              jnp.concatenate([kd, bd], axis=0), 0, 0), 0.0),
              v_c, taw, kd_c, bd_c)

    states = [s_ref[b] for b in range(NB)]
    ys = [[] for _ in range(NB)]
    for ch in range(nc):
        for b in range(NB):
            i, s = b * nc + ch, states[b]
            ys[b].append(_dot(rq[i], s, 1, 1) + y0[i])
            states[b] = s * gamma_end[i][C - 1:C, :] + _dot(s, g[i]) + s0[i]
    for b in range(NB):
        s_ref[b] = states[b]
        y = jnp.concatenate(ys[b], axis=0)
        mu = headsum(y) * (1.0 / HEAD_DIM)
        yc = y - mu
        var = headsum(yc * yc) * (1.0 / HEAD_DIM)
        yn = yc * lax.rsqrt(var + RWKV_GN_EPS) * ln_w + ln_b
        o_ref[b] = ((yn + fronts[b][1]) * _silu(g_ref[b])).astype(o_ref.dtype)


def _rwkv(p3, par, mixwa, wup, aup, l, col0, gcol0, npairs, tt):
    B, T, _ = p3.shape
    col = lambda off: pl.BlockSpec((B, tt, LANES), lambda j, t: (0, t, off + j))
    fixed = lambda off: pl.BlockSpec((B, tt, LANES), lambda j, t: (0, t, off))
    return pl.pallas_call(
        _rwkv_body,
        grid=(npairs, T // tt),
        in_specs=[col(col0), col(col0 + npairs), col(col0 + 2 * npairs), fixed(col0 + 3 * npairs),
                  col(gcol0),
                  _layer(l, 16, LANES)(lambda j, t: (0, j)),
                  _layer(l, 1, LANES)(lambda j, t: (0, 0)),
                  _layer(l, LANES, LANES)(lambda j, t: (0, j)),
                  _layer(l, LANES, LANES)(lambda j, t: (0, j))],
        out_specs=pl.BlockSpec((B, tt, LANES), lambda j, t: (0, t, j)),
        out_shape=jax.ShapeDtypeStruct((B, T, npairs * LANES), BF16),
        scratch_shapes=[pltpu.VMEM((4 * B, tt + 8, LANES), F32), pltpu.VMEM((B, LANES, LANES), F32)],
        compiler_params=pltpu.CompilerParams(
            dimension_semantics=("parallel", "arbitrary"), vmem_limit_bytes=VMEM_LIMIT),
        name="rwkv7",
    )(p3, p3, p3, p3, p3, par, mixwa, wup, aup)


def _moba_body(q_ref, k_ref, v_ref, g_ref, bown_ref, bprev_ref, o_ref, kmean_ref, kb_ref, vt_ref, acc_ref,
               sc_ref, p_ref):
    BS = MOBA_BLOCK
    n_batch, T = k_ref.shape[0], kb_ref.shape[1]
    NB = n_batch * (k_ref.shape[2] // LANES)
    nb = T // BS
    i = pl.program_id(1)
    lane = lax.broadcasted_iota(jnp.int32, (1, LANES), 1)
    head0 = lane < HEAD_DIM
    streams = [(b, h) for b in range(NB) for h in range(2)]

    def tile(ref, u):
        return ref[u % n_batch][:, (u // n_batch) * LANES:(u // n_batch + 1) * LANES]

    def table(ref, bh):
        return ref[2 * (bh[0] // n_batch) + bh[1]]

    def each(f, *lists):
        return [f(*xs) for xs in zip(*lists)]

    VR = MOBA_VROWS

    @pl.when(i == 0)
    def _():
        kmean_ref[...] = jnp.zeros(kmean_ref.shape, F32)

    own_rows = pl.ds(pl.multiple_of(i * BS, BS), BS)
    ones = jnp.ones((VR - HEAD_DIM, BS), BF16)
    is_own = lax.broadcasted_iota(jnp.int32, (nb, 1), 0) == i
    for b in range(NB):
        k_blk = tile(k_ref, b)
        kmean_ref[b] = jnp.where(is_own, jnp.mean(k_blk, axis=0, keepdims=True), kmean_ref[b])
        kb_ref[b, own_rows, :] = k_blk.astype(BF16)
        vt = tile(v_ref, b).T.astype(BF16)
        vt_ref[b, i] = jnp.concatenate([vt[0:HEAD_DIM], ones, vt[HEAD_DIM:2 * HEAD_DIM], ones], axis=0)

    qh = []
    for b in range(NB):
        q = tile(q_ref, b)
        qh += [jnp.where(head0, q, 0.0), jnp.where(head0, 0.0, q)]
    q_scaled = [x * (HEAD_DIM ** -0.5 * LOG2_E) for x in qh]
    qs = [x.astype(BF16) for x in q_scaled]

    def scores(j):
        rows = pl.ds(pl.multiple_of(j * BS, BS), BS)
        return [_dot(kb_ref[b, rows, :], qs[s], 1, 1) for s, (b, _) in enumerate(streams)]

    def weighted(j, s, p):
        b, h = streams[s]
        return _dot(vt_ref[b, j, h * VR:(h + 1) * VR, :], p)

    def acc_at(s):
        b, h = streams[s]
        return b, slice(h * VR, (h + 1) * VR)

    def split(x):
        hi = x.astype(BF16)
        return hi, (x - hi.astype(F32)).astype(BF16)

    prev_blk = jnp.maximum(i - 1, 0)
    kmean_parts = [split(kmean_ref[b]) for b in range(NB)]
    own = [_dot(jnp.concatenate([kb_ref[b, own_rows, :], kmean_parts[b][0], kmean_parts[b][1]], axis=0), qs[s], 1, 1)
           for s, (b, _) in enumerate(streams)]
    s_own = each(lambda x, bh: x[0:BS] + table(bown_ref, bh), own, streams)
    gate = [x[BS:BS + nb] + x[BS + nb:BS + 2 * nb]
            + _dot(kmean_parts[b][0], (xq - q.astype(F32)).astype(BF16), 1, 1)
            for x, xq, q, (b, _) in zip(own, q_scaled, qs, streams)]
    s_prev = each(lambda x, bh: x + table(bprev_ref, bh), scores(prev_blk), streams)
    for s, x in enumerate(scores(0)):
        sc_ref[s] = x

    m = each(lambda x: jnp.max(x, axis=0, keepdims=True), s_own)
    for s in range(len(streams)):
        b, rows = acc_at(s)
        acc_ref[b, rows, :] = weighted(i, s, jnp.exp2(s_own[s] - m[s]))

    blk = lax.broadcasted_iota(jnp.int32, (nb, 1), 0).astype(F32)
    i_f = i.astype(F32)
    gate = each(lambda x: jnp.where(blk < i_f, x, -jnp.inf), gate)
    sel = []
    for rnk in range(MOBA_TOPK):
        top = each(lambda x: jnp.max(x, axis=0, keepdims=True), gate)
        idx = each(lambda x, tp: jnp.min(jnp.where(x == tp, blk, float(nb)), axis=0, keepdims=True), gate, top)
        gate = each(lambda x, ix: jnp.where(blk == ix, -jnp.inf, x), gate, idx)
        sel.append(each(lambda ix: jnp.where(rnk < i, ix, -5.0), idx))

    def chosen(s, j):
        j_f = j.astype(F32)
        return (sel[0][s] == j_f) | (sel[1][s] == j_f) | (sel[2][s] == j_f)

    NS = len(streams)

    def softmax_step(j, sc, m, valid, slot):
        m_new, alpha = [], []
        for s in range(len(streams)):
            take = chosen(s, j) & valid
            top = jnp.max(sc[s], axis=0, keepdims=True)
            mn = jnp.maximum(m[s], jnp.where(take, top, -jnp.inf))
            p_ref[slot * NS + s] = jnp.exp2(sc[s] - jnp.where(take, mn, jnp.inf)).astype(BF16)
            m_new.append(mn)
            alpha.append(jnp.exp2(m[s] - mn))
        return m_new, alpha

    def accumulate(j, alpha, slot):
        for s in range(len(streams)):
            b, rows = acc_at(s)
            acc_ref[b, rows, :] = alpha[s] * acc_ref[b, rows, :] + weighted(j, s, p_ref[slot * NS + s])

    m, alpha = softmax_step(prev_blk, s_prev, m, i >= 1, 0)

    def far_block(j, st):
        m, alpha = list(st[0]), list(st[1])
        cur, nxt_slot = j & 1, (j + 1) & 1
        nxt = scores(jnp.minimum(j + 1, nb - 1))
        accumulate(jnp.where(j == 0, prev_blk, j - 1), alpha, cur)
        m, alpha = softmax_step(j, [sc_ref[cur * NS + s] for s in range(NS)], m, True, nxt_slot)
        for s, x in enumerate(nxt):
            sc_ref[nxt_slot * NS + s] = x
        return tuple(m), tuple(alpha)

    n_far = jnp.maximum(i - 1, 0)
    m, alpha = lax.fori_loop(0, n_far, far_block, (tuple(m), tuple(alpha)))
    accumulate(jnp.where(n_far == 0, prev_blk, n_far - 1), alpha, n_far & 1)

    outs = []
    for b in range(NB):
        acc = acc_ref[b]
        out_t = jnp.concatenate([acc[h * VR:h * VR + HEAD_DIM] * (1.0 / acc[h * VR + HEAD_DIM:h * VR + HEAD_DIM + 1])
                                 for h in range(2)], axis=0)
        outs.append(out_t.T * _silu(tile(g_ref, b)))
    for b in range(n_batch):
        o_ref[b] = jnp.concatenate(outs[b::n_batch], axis=1).astype(o_ref.dtype)


def _moba(p3, bown, bprev, col0, gcol0, npairs, tiles):
    B, T, _ = p3.shape
    BS = MOBA_BLOCK
    nb = T // BS
    W = tiles * LANES
    assert npairs % tiles == 0 and col0 % tiles == 0 and gcol0 % tiles == 0
    groups = npairs // tiles
    nseq = tiles * B
    return pl.pallas_call(
        _moba_body,
        grid=(groups, nb),
        in_specs=[pl.BlockSpec((B, BS, W), lambda j, i: (0, i, col0 // tiles + j)),
                  pl.BlockSpec((B, BS, W), lambda j, i: (0, i, col0 // tiles + groups + j)),
                  pl.BlockSpec((B, BS, W), lambda j, i: (0, i, col0 // tiles + 2 * groups + j)),
                  pl.BlockSpec((B, BS, W), lambda j, i: (0, i, gcol0 // tiles + j)),
                  pl.BlockSpec((2 * tiles, BS, BS), lambda j, i: (j, 0, 0)),
                  pl.BlockSpec((2 * tiles, BS, BS), lambda j, i: (j, 0, 0))],
        out_specs=pl.BlockSpec((B, BS, W), lambda j, i: (0, i, j)),
        out_shape=jax.ShapeDtypeStruct((B, T, npairs * LANES), BF16),
        scratch_shapes=[pltpu.VMEM((nseq, nb, LANES), F32), pltpu.VMEM((nseq, T, LANES), BF16),
                        pltpu.VMEM((nseq, nb, 2 * MOBA_VROWS, BS), BF16), pltpu.VMEM((nseq, 2 * MOBA_VROWS, BS), F32),
                        pltpu.VMEM((4 * nseq, BS, BS), F32), pltpu.VMEM((4 * nseq, BS, BS), BF16)],
        compiler_params=pltpu.CompilerParams(
            dimension_semantics=("parallel", "arbitrary"), vmem_limit_bytes=VMEM_LIMIT),
        name="moba",
    )(p3, p3, p3, p3, bown, bprev)


def _rel_bucket_of(dist):
    max_exact = REL_BUCKETS // 2
    large = max_exact + (jnp.log(jnp.maximum(dist, 1).astype(F32) / max_exact)
                         / math.log(REL_MAX_DIST / max_exact) * (REL_BUCKETS - max_exact)).astype(jnp.int32)
    large = jnp.minimum(large, REL_BUCKETS - 1)
    return jnp.where(dist < max_exact, dist, large)


def _toeplitz_kq(tab):
    H, L = tab.shape
    BS = L // 2
    flat = jnp.tile(jnp.roll(tab, -1, axis=1), (1, BS))[:, :BS * (L - 1)]
    return flat.reshape(H, BS, L - 1)[:, :, BS - 1:2 * BS - 1]


def _moba_bias_tables(rel_bias):
    BS = MOBA_BLOCK
    per_dist = rel_bias.astype(F32)[_rel_bucket_of(jnp.arange(2 * BS))].T
    per_dist = (per_dist - per_dist[:, 2 * BS - 1:2 * BS]) * LOG2_E
    prev = _toeplitz_kq(per_dist)
    ki = np.arange(BS)[:, None]
    qi = np.arange(BS)[None, :]
    own = jnp.where(ki <= qi, _toeplitz_kq(jnp.roll(per_dist, BS, axis=1)), -jnp.inf)
    return own, prev


def _block_diag(w):
    d, g, n, _ = w.shape
    return jnp.einsum('lgij,gh->lgihj', w, jnp.eye(g, dtype=w.dtype)).reshape(d, g * n, g * n)


def kernel(x, norm_w, w_in, w_out, lru_conv_w, lru_conv_b, lru_gate_a_w, lru_gate_a_b, lru_gate_x_w, lru_gate_x_b, lru_lambda, rwkv_mix, rwkv_w0, rwkv_w_up, rwkv_a0, rwkv_a_up, rwkv_k_k, rwkv_k_a, rwkv_r_k, rwkv_ln_w, rwkv_ln_b, rel_bias, final_norm_w):
    B, T, D = x.shape
    depth = w_in.shape[0]
    lru_w = lru_conv_w.shape[2]
    rwkv_w = rwkv_w0.shape[1]
    moba_w = rel_bias.shape[1] * HEAD_DIM
    rwkv_pairs = rwkv_w // LANES
    moba_pairs = moba_w // LANES
    rwkv_col = (2 * lru_w) // LANES
    rwkv_gate_col = rwkv_col + (3 * rwkv_w + 2 * RWKV_LORA) // LANES
    moba_col = rwkv_gate_col + rwkv_w // LANES
    moba_gate_col = moba_col + (3 * moba_w) // LANES
    n_cols = w_in.shape[2]
    assert (moba_gate_col + moba_w // LANES) * LANES == n_cols
    assert T % MOBA_BLOCK == 0 and 2 * RWKV_LORA == LANES

    M = B * T
    tm_in = 1024
    tn_in = n_cols // 3
    tm_out = 512
    tt_lru = 512
    tt_rwkv = 1024
    moba_tiles = 3

    bown, bprev = _moba_bias_tables(rel_bias)
    lru_vec = jnp.stack([lru_conv_b, lru_gate_a_b, lru_gate_x_b, lru_lambda] + [jnp.zeros_like(lru_lambda)] * 4,
                        axis=1)
    lru_wa = _block_diag(lru_gate_a_w).astype(BF16)
    lru_wx = _block_diag(lru_gate_x_w).astype(BF16)
    mix = rwkv_mix
    rwkv_par = jnp.stack([rwkv_w0, rwkv_a0, rwkv_k_k, rwkv_k_a, rwkv_ln_w, rwkv_ln_b,
                          rwkv_r_k.reshape(depth, rwkv_w), mix[:, 0:rwkv_w], mix[:, rwkv_w:2 * rwkv_w],
                          mix[:, 2 * rwkv_w:3 * rwkv_w]] + [jnp.zeros_like(rwkv_w0)] * 6, axis=1)
    mixwa = mix[:, 3 * rwkv_w:].reshape(depth, 1, LANES)
    zeros_lora = jnp.zeros_like(rwkv_w_up)
    wup = jnp.concatenate([rwkv_w_up, zeros_lora], axis=1).astype(BF16)
    aup = jnp.concatenate([zeros_lora, rwkv_a_up], axis=1).astype(BF16)

    x2 = x.reshape(M, D)
    h, w_in_b, w_out_b = _norm_cast(x2, norm_w[0].reshape(1, D), w_in, w_out, tm_out)
    for l in range(depth):
        final = l == depth - 1
        l_next = None if final else l + 1
        p = _in_proj(h, w_in_b, w_in, l_next, tm_in, tn_in)
        if not final:
            p, w_in_b = p
        p3 = p.reshape(B, T, n_cols)
        ya = _lru(p3, lru_conv_w, lru_vec, lru_wa, lru_wx, l, lru_w, tt_lru)
        yb = _rwkv(p3, rwkv_par, mixwa, wup, aup, l, rwkv_col, rwkv_gate_col, rwkv_pairs, tt_rwkv)
        yc = _moba(p3, bown, bprev, moba_col, moba_gate_col, moba_pairs, moba_tiles)
        nw = (final_norm_w if final else norm_w[l + 1]).reshape(1, D)
        res = _out_proj(x2, ya.reshape(M, lru_w), yb.reshape(M, rwkv_w), yc.reshape(M, moba_w),
                        w_out_b, nw, w_out, l_next, tm_out)
        if final:
            return res.reshape(B, T, D)
        x2, h, w_out_b = res
```

```python
import functools
import math

import numpy as np
import jax
import jax.numpy as jnp
from jax import lax
from jax.experimental import pallas as pl
from jax.experimental.pallas import tpu as pltpu

F32 = jnp.float32
BF16 = jnp.bfloat16

LANES = 128
HEAD_DIM = 64
NORM_EPS = 1e-6
LRU_CONV = 4
LRU_C = 8.0
RWKV_LORA = 64
RWKV_DECAY_SCALE = math.exp(-0.5)
RWKV_GN_EPS = 64e-5
RWKV_CHUNK = 64
MOBA_BLOCK = 256
MOBA_TOPK = 3
MOBA_VROWS = 80
REL_BUCKETS = 32
REL_MAX_DIST = 128
LOG2_E = math.log2(math.e)
VMEM_LIMIT = 60 * 1024 * 1024


def _dot(a, b, ca=1, cb=0):
    return lax.dot_general(a.astype(BF16), b.astype(BF16), (((ca,), (cb,)), ((), ())),
                           preferred_element_type=F32)


def _sigmoid(x):
    return 0.5 * jnp.tanh(0.5 * x) + 0.5


def _silu(x):
    return x * _sigmoid(x)


def _rms_norm(x, w):
    ms = jnp.mean(x * x, axis=-1, keepdims=True)
    return x * lax.rsqrt(ms + NORM_EPS) * w


def _norm_body(x_ref, nw_ref, wi_ref, wo_ref, h_ref, wib_ref, wob_ref):
    h_ref[...] = _rms_norm(x_ref[...], nw_ref[...]).astype(BF16)
    wib_ref[...] = wi_ref[...].astype(BF16)
    wob_ref[...] = wo_ref[...].astype(BF16)


def _slab_rows(D, steps):
    rows = D // steps
    assert rows * steps == D and rows % 16 == 0
    return rows


def _norm_cast(x2, nw, w_in, w_out, tm):
    M, D = x2.shape
    N = w_in.shape[2]
    rows = _slab_rows(D, M // tm)
    return pl.pallas_call(
        _norm_body,
        grid=(M // tm,),
        in_specs=[pl.BlockSpec((tm, D), lambda i: (i, 0)), pl.BlockSpec((1, D), lambda i: (0, 0)),
                  _layer(0, rows, N)(lambda i: (i, 0)), _layer(0, rows, D)(lambda i: (i, 0))],
        out_specs=[pl.BlockSpec((tm, D), lambda i: (i, 0)), pl.BlockSpec((rows, N), lambda i: (i, 0)),
                   pl.BlockSpec((rows, D), lambda i: (i, 0))],
        out_shape=[jax.ShapeDtypeStruct((M, D), BF16), jax.ShapeDtypeStruct((D, N), BF16),
                   jax.ShapeDtypeStruct((D, D), BF16)],
        compiler_params=pltpu.CompilerParams(dimension_semantics=("parallel",), vmem_limit_bytes=VMEM_LIMIT),
        name="norm_cast",
    )(x2, nw, w_in, w_out)


def _layer(l, *block):
    def spec(tail):
        return pl.BlockSpec((None,) + block, lambda *g: (l,) + tail(*g))
    return spec


def _inproj_body(h_ref, w_ref, *rest):
    if len(rest) == 3:
        wn_ref, o_ref, wnb_ref = rest
        wnb_ref[...] = wn_ref[...].astype(BF16)
    else:
        (o_ref,) = rest
    o_ref[...] = jnp.dot(h_ref[...], w_ref[...], preferred_element_type=F32)


def _in_proj(h, w, w_all, l_next, tm, tn):
    M, D = h.shape
    N = w.shape[1]
    in_specs = [pl.BlockSpec((tm, D), lambda j, i: (i, 0)), pl.BlockSpec((D, tn), lambda j, i: (0, j))]
    out_specs = pl.BlockSpec((tm, tn), lambda j, i: (i, j))
    out_shape = jax.ShapeDtypeStruct((M, N), F32)
    args = (h, w)
    if l_next is not None:
        rows = _slab_rows(D, M // tm)
        in_specs.append(_layer(l_next, rows, tn)(lambda j, i: (i, j)))
        out_specs = [out_specs, pl.BlockSpec((rows, tn), lambda j, i: (i, j))]
        out_shape = [out_shape, jax.ShapeDtypeStruct((D, N), BF16)]
        args += (w_all,)
    return pl.pallas_call(
        _inproj_body,
        grid=(N // tn, M // tm),
        in_specs=in_specs,
        out_specs=out_specs,
        out_shape=out_shape,
        compiler_params=pltpu.CompilerParams(
            dimension_semantics=("parallel", "parallel"), vmem_limit_bytes=VMEM_LIMIT),
        name="in_proj" if l_next is not None else "in_proj_last",
    )(*args)


def _outproj_body(x_ref, ya_ref, yb_ref, yc_ref, w_ref, nw_ref, *rest, final):
    wa, wb = ya_ref.shape[1], yb_ref.shape[1]
    acc = x_ref[...]
    acc = acc + _dot(ya_ref[...], w_ref[0:wa, :])
    acc = acc + _dot(yb_ref[...], w_ref[wa:wa + wb, :])
    acc = acc + _dot(yc_ref[...], w_ref[wa + wb:, :])
    normed = _rms_norm(acc, nw_ref[...])
    if final:
        rest[0][...] = normed
    else:
        wn_ref, x_out_ref, h_ref, wnb_ref = rest
        x_out_ref[...] = acc
        h_ref[...] = normed.astype(BF16)
        wnb_ref[...] = wn_ref[...].astype(BF16)


def _out_proj(x2, ya, yb, yc, w, nw, w_all, l_next, tm):
    M, D = x2.shape
    final = l_next is None
    row = lambda width: pl.BlockSpec((tm, width), lambda i: (i, 0))
    full = lambda a: pl.BlockSpec(a.shape, lambda i: (0, 0))
    in_specs = [row(D), row(ya.shape[1]), row(yb.shape[1]), row(yc.shape[1]), full(w), full(nw)]
    args = (x2, ya, yb, yc, w, nw)
    if final:
        out_specs, out_shape = row(D), jax.ShapeDtypeStruct((M, D), F32)
    else:
        rows = _slab_rows(D, M // tm)
        in_specs.append(_layer(l_next, rows, D)(lambda i: (i, 0)))
        args += (w_all,)
        out_specs = [row(D), row(D), pl.BlockSpec((rows, D), lambda i: (i, 0))]
        out_shape = [jax.ShapeDtypeStruct((M, D), F32), jax.ShapeDtypeStruct((M, D), BF16),
                     jax.ShapeDtypeStruct((D, D), BF16)]
    return pl.pallas_call(
        functools.partial(_outproj_body, final=final),
        grid=(M // tm,),
        in_specs=in_specs,
        out_specs=out_specs,
        out_shape=out_shape,
        compiler_params=pltpu.CompilerParams(
            dimension_semantics=("parallel",), vmem_limit_bytes=VMEM_LIMIT),
        name="out_proj_final" if final else "out_proj",
    )(*args)


def _lru_body(xa_ref, ga_ref, cw_ref, vec_ref, wa_ref, wx_ref, o_ref, xbuf, hcar):
    tt = xa_ref.shape[1]
    W = xa_ref.shape[2]

    @pl.when(pl.program_id(1) == 0)
    def _():
        xbuf[0:8, :] = jnp.zeros((8, W), F32)
        hcar[...] = jnp.zeros((8, W), F32)

    xa = xa_ref[0]
    xbuf[8:8 + tt, :] = xa
    xc = vec_ref[0:1, :] + cw_ref[LRU_CONV - 1:LRU_CONV, :] * xa
    for s in range(1, LRU_CONV):
        xc = xc + cw_ref[LRU_CONV - 1 - s:LRU_CONV - s, :] * xbuf[8 - s:8 - s + tt, :]
    xbuf[0:8, :] = xa[tt - 8:tt, :]

    r = _sigmoid(_dot(xc, wa_ref[...]) + vec_ref[1:2, :])
    gi = _sigmoid(_dot(xc, wx_ref[...]) + vec_ref[2:3, :])
    lam = vec_ref[3:4, :]
    softplus_neg_lam = jnp.maximum(-lam, 0.0) + jnp.log1p(jnp.exp(-jnp.abs(lam)))
    log_a = (-LRU_C) * r * softplus_neg_lam
    a = jnp.exp(log_a)
    one_minus_a2 = -jnp.tanh(log_a) * (a * a + 1.0)
    b = jnp.where(one_minus_a2 > 0.0, one_minus_a2 * lax.rsqrt(one_minus_a2), 0.0) * (gi * xc)

    row = lax.broadcasted_iota(jnp.int32, (tt, 1), 0)
    d = 1
    while d < tt:
        keep = row >= d
        a_sh = pltpu.roll(a, d, 0)
        b_sh = pltpu.roll(b, d, 0)
        b = b + jnp.where(keep, a * b_sh, 0.0)
        a = jnp.where(keep, a * a_sh, a)
        d *= 2
    h = b + a * hcar[0:1, :]
    hcar[...] = jnp.broadcast_to(h[tt - 1:tt, :], (8, W))
    o_ref[0] = (h * _silu(ga_ref[0])).astype(o_ref.dtype)


def _lru(p3, cw, vec, wa, wx, l, width, tt):
    B, T, _ = p3.shape
    full = lambda a: _layer(l, *a.shape[1:])(lambda b, t: (0, 0))
    return pl.pallas_call(
        _lru_body,
        grid=(B, T // tt),
        in_specs=[pl.BlockSpec((1, tt, width), lambda b, t: (b, t, 0)),
                  pl.BlockSpec((1, tt, width), lambda b, t: (b, t, 1)),
                  full(cw), full(vec), full(wa), full(wx)],
        out_specs=pl.BlockSpec((1, tt, width), lambda b, t: (b, t, 0)),
        out_shape=jax.ShapeDtypeStruct((B, T, width), BF16),
        scratch_shapes=[pltpu.VMEM((tt + 8, width), F32), pltpu.VMEM((8, width), F32)],
        compiler_params=pltpu.CompilerParams(
            dimension_semantics=("parallel", "arbitrary"), vmem_limit_bytes=VMEM_LIMIT),
        name="rg_lru",
    )(p3, p3, cw, vec, wa, wx)


def _rwkv_body(r_ref, k_ref, v_ref, wa_ref, g_ref, par_ref, mixwa_ref, wup_ref, aup_ref, o_ref,
               xs_ref, s_ref):
    C = RWKV_CHUNK
    NB, tt = r_ref.shape[0], r_ref.shape[1]
    nc = tt // C

    @pl.when(pl.program_id(1) == 0)
    def _():
        xs_ref[:, 0:8, :] = jnp.zeros((xs_ref.shape[0], 8, LANES), F32)
        s_ref[...] = jnp.zeros(s_ref.shape, F32)

    lane = lax.broadcasted_iota(jnp.int32, (1, LANES), 1)
    head0 = lane < HEAD_DIM
    rc = lax.broadcasted_iota(jnp.int32, (tt, 1), 0) & (C - 1)

    par = par_ref[...]
    w0, a0, k_k, k_a = par[0:1], par[1:2], par[2:3], par[3:4]
    ln_w, ln_b, r_k = par[4:5], par[5:6], par[6:7]
    mix_r, mix_k, mix_v = par[7:8], par[8:9], par[9:10]

    def shift_lerp(ref, b, mix, slot):
        s = ref[b]
        xs_ref[4 * b + slot, 8:8 + tt, :] = s
        prev = xs_ref[4 * b + slot, 7:7 + tt, :]
        xs_ref[4 * b + slot, 0:8, :] = s[tt - 8:tt, :]
        return s + mix * (prev - s)

    def headsum(x):
        s0 = jnp.sum(jnp.where(head0, x, 0.0), axis=-1, keepdims=True)
        s1 = jnp.sum(jnp.where(head0, 0.0, x), axis=-1, keepdims=True)
        return jnp.where(head0, s0, s1)

    def chunks(x):
        return [x[ch * C:(ch + 1) * C] for ch in range(nc)]

    def front(b):
        r = shift_lerp(r_ref, b, mix_r, 0)
        k = shift_lerp(k_ref, b, mix_k, 1)
        v = shift_lerp(v_ref, b, mix_v, 2)
        wa = shift_lerp(wa_ref, b, mixwa_ref[...], 3)
        log_w = (-RWKV_DECAY_SCALE) * _sigmoid(w0 + _dot(jnp.tanh(wa), wup_ref[...]))
        a = _sigmoid(a0 + _dot(wa, aup_ref[...]))
        kk = k * k_k
        kk = kk * lax.rsqrt(jnp.maximum(headsum(kk * kk), 1e-24))
        k = k * (1.0 + (a - 1.0) * k_a)
        be = kk * a
        c = log_w
        d = 1
        while d < C:
            c = c + jnp.where(rc >= d, pltpu.roll(c, d, 0), 0.0)
            d *= 2
        c_last = jnp.broadcast_to(c.reshape(nc, C, LANES)[:, C - 1:C, :], (nc, C, LANES)).reshape(tt, LANES)
        e_inc = jnp.exp(c)
        e_inv = jnp.exp(-c)
        e_end = jnp.exp(c_last - c)
        parts = dict(a=-kk * jnp.exp(c - log_w), r=r * e_inc, kt=k * e_inv, bt=be * e_inv,
                     kd=k * e_end, bd=be * e_end, v=v, gamma_end=jnp.exp(c_last))
        return {name: chunks(val) for name, val in parts.items()}, headsum(r * k * r_k) * v

    fronts = [front(b) for b in range(NB)]
    gather = lambda name: [x for f, _ in fronts for x in f[name]]

    ti = lax.broadcasted_iota(jnp.int32, (C, LANES), 0)
    si = lax.broadcasted_iota(jnp.int32, (C, LANES), 1) & (C - 1)
    strict = ti > si
    incl = ti >= si
    same16 = (ti >> 4) == (si >> 4)
    same32 = (ti >> 5) == (si >> 5)
    eye = (ti == si).astype(F32)
    same_head = ((lax.broadcasted_iota(jnp.int32, (LANES, LANES), 0) >> 6)
                 == (lax.broadcasted_iota(jnp.int32, (LANES, LANES), 1) >> 6))
    head0_b = jnp.broadcast_to(head0, (C, LANES))

    def embed(x):
        xb = x.astype(BF16)
        zero = jnp.zeros_like(xb)
        return jnp.concatenate([jnp.where(head0_b, xb, zero), jnp.where(head0_b, zero, xb)], axis=0)

    def each(f, *lists):
        return [f(*xs) for xs in zip(*lists)]

    lo = lambda x: x[:, 0:LANES]
    hi = lambda x: x[:, LANES:2 * LANES]
    side = lambda x, y: jnp.concatenate([x, y], axis=1)

    a_c = gather('a')
    r_c = gather('r')
    k_e = each(embed, gather('kt'))
    b_e = each(embed, gather('bt'))
    kd_c = gather('kd')
    bd_c = gather('bd')
    v_c = gather('v')
    v_e = each(embed, v_c)
    gamma_end = gather('gamma_end')

    p = each(lambda aa, rr, kt, bt: _dot(jnp.concatenate([aa, rr], axis=0),
                                         jnp.concatenate([kt, bt], axis=0), 1, 1), a_c, r_c, k_e, b_e)
    a_ak = each(lambda x: jnp.where(strict, x[0:C, 0:LANES], 0.0), p)
    n = each(lambda x: jnp.where(strict, x[0:C, LANES:2 * LANES], 0.0), p)
    a_rk = each(lambda x: jnp.where(incl, x[C:2 * C, 0:LANES], 0.0), p)
    a_rb = each(lambda x: jnp.where(incl, x[C:2 * C, LANES:2 * LANES], 0.0), p)
    q = each(lambda x: jnp.where(same16, x, 0.0), n)
    t = each(lambda x: eye + x, q)
    q = each(lambda x: _dot(x, embed(x)), q)
    for _ in range(2):
        qt = each(lambda x, y: _dot(x, side(embed(y), embed(x))), q, t)
        t = each(lambda x, y: x + lo(y), t, qt)
        q = each(hi, qt)
    t = each(lambda x, y: x + _dot(y, embed(x)), t, q)
    tn = each(lambda x, y: _dot(x, embed(jnp.where(same32 & jnp.logical_not(same16), y, 0.0))), t, n)
    t = each(lambda x, y: x + _dot(y, embed(x)), t, tn)
    tn = each(lambda x, y: _dot(x, embed(jnp.where(same32, 0.0, y))), t, n)
    t = each(lambda x, y: x + _dot(y, embed(x)), t, tn)
    av = each(lambda x, y, vv: _dot(jnp.concatenate([x, y], axis=0), vv), a_ak, a_rk, v_e)
    taw = each(lambda x, aa, w: _dot(x, side(embed(aa), embed(w[0:C]))), t, a_c, av)
    ry = each(lambda x, y: _dot(x, side(embed(lo(y)), embed(hi(y)))), a_rb, taw)
    rq = each(lambda rr, x: rr + lo(x), r_c, ry)
    y0 = each(lambda w, x: w[C:2 * C] + hi(x), av, ry)
    g = each(lambda x, bd: jnp.where(same_head, _dot(lo(x), bd, 0, 0), 0.0), taw, bd_c)
    s0 = each(lambda vv, x, kd, bd: jnp.where(same_head, _dot(jnp.concatenate([vv, hi(x)], axis=0),
                                                              jnp.concatenate([kd, bd], axis=0), 0, 0), 0.0),
              v_c, taw, kd_c, bd_c)

    states = [s_ref[b] for b in range(NB)]
    ys = [[] for _ in range(NB)]
    for ch in range(nc):
        for b in range(NB):
            i, s = b * nc + ch, states[b]
            ys[b].append(_dot(rq[i], s, 1, 1) + y0[i])
            states[b] = s * gamma_end[i][C - 1:C, :] + _dot(s, g[i]) + s0[i]
    for b in range(NB):
        s_ref[b] = states[b]
        y = jnp.concatenate(ys[b], axis=0)
        mu = headsum(y) * (1.0 / HEAD_DIM)
        yc = y - mu
        var = headsum(yc * yc) * (1.0 / HEAD_DIM)
        yn = yc * lax.rsqrt(var + RWKV_GN_EPS) * ln_w + ln_b
        o_ref[b] = ((yn + fronts[b][1]) * _silu(g_ref[b])).astype(o_ref.dtype)


def _rwkv(p3, par, mixwa, wup, aup, l, col0, gcol0, npairs, tt):
    B, T, _ = p3.shape
    col = lambda off: pl.BlockSpec((B, tt, LANES), lambda j, t: (0, t, off + j))
    fixed = lambda off: pl.BlockSpec((B, tt, LANES), lambda j, t: (0, t, off))
    return pl.pallas_call(
        _rwkv_body,
        grid=(npairs, T // tt),
        in_specs=[col(col0), col(col0 + npairs), col(col0 + 2 * npairs), fixed(col0 + 3 * npairs),
                  col(gcol0),
                  _layer(l, 16, LANES)(lambda j, t: (0, j)),
                  _layer(l, 1, LANES)(lambda j, t: (0, 0)),
                  _layer(l, LANES, LANES)(lambda j, t: (0, j)),
                  _layer(l, LANES, LANES)(lambda j, t: (0, j))],
        out_specs=pl.BlockSpec((B, tt, LANES), lambda j, t: (0, t, j)),
        out_shape=jax.ShapeDtypeStruct((B, T, npairs * LANES), BF16),
        scratch_shapes=[pltpu.VMEM((4 * B, tt + 8, LANES), F32), pltpu.VMEM((B, LANES, LANES), F32)],
        compiler_params=pltpu.CompilerParams(
            dimension_semantics=("parallel", "arbitrary"), vmem_limit_bytes=VMEM_LIMIT),
        name="rwkv7",
    )(p3, p3, p3, p3, p3, par, mixwa, wup, aup)


def _moba_body(q_ref, k_ref, v_ref, g_ref, bown_ref, bprev_ref, o_ref, kmean_ref, kb_ref, vt_ref, acc_ref,
               sc_ref, p_ref):
    BS = MOBA_BLOCK
    n_batch, T = k_ref.shape[0], kb_ref.shape[1]
    NB = n_batch * (k_ref.shape[2] // LANES)
    nb = T // BS
    i = pl.program_id(1)
    lane = lax.broadcasted_iota(jnp.int32, (1, LANES), 1)
    head0 = lane < HEAD_DIM
    streams = [(b, h) for b in range(NB) for h in range(2)]

    def tile(ref, u):
        return ref[u % n_batch][:, (u // n_batch) * LANES:(u // n_batch + 1) * LANES]

    def table(ref, bh):
        return ref[2 * (bh[0] // n_batch) + bh[1]]

    def each(f, *lists):
        return [f(*xs) for xs in zip(*lists)]

    VR = MOBA_VROWS

    @pl.when(i == 0)
    def _():
        kmean_ref[...] = jnp.zeros(kmean_ref.shape, F32)

    own_rows = pl.ds(pl.multiple_of(i * BS, BS), BS)
    ones = jnp.ones((VR - HEAD_DIM, BS), BF16)
    is_own = lax.broadcasted_iota(jnp.int32, (nb, 1), 0) == i
    for b in range(NB):
        k_blk = tile(k_ref, b)
        kmean_ref[b] = jnp.where(is_own, jnp.mean(k_blk, axis=0, keepdims=True), kmean_ref[b])
        kb_ref[b, own_rows, :] = k_blk.astype(BF16)
        vt = tile(v_ref, b).T.astype(BF16)
        vt_ref[b, i] = jnp.concatenate([vt[0:HEAD_DIM], ones, vt[HEAD_DIM:2 * HEAD_DIM], ones], axis=0)

    qh = []
    for b in range(NB):
        q = tile(q_ref, b)
        qh += [jnp.where(head0, q, 0.0), jnp.where(head0, 0.0, q)]
    q_scaled = [x * (HEAD_DIM ** -0.5 * LOG2_E) for x in qh]
    qs = [x.astype(BF16) for x in q_scaled]

    def scores(j):
        rows = pl.ds(pl.multiple_of(j * BS, BS), BS)
        return [_dot(kb_ref[b, rows, :], qs[s], 1, 1) for s, (b, _) in enumerate(streams)]

    def weighted(j, s, p):
        b, h = streams[s]
        return _dot(vt_ref[b, j, h * VR:(h + 1) * VR, :], p)

    def acc_at(s):
        b, h = streams[s]
        return b, slice(h * VR, (h + 1) * VR)

    def split(x):
        hi = x.astype(BF16)
        return hi, (x - hi.astype(F32)).astype(BF16)

    prev_blk = jnp.maximum(i - 1, 0)
    kmean_parts = [split(kmean_ref[b]) for b in range(NB)]
    own = [_dot(jnp.concatenate([kb_ref[b, own_rows, :], kmean_parts[b][0], kmean_parts[b][1]], axis=0), qs[s], 1, 1)
           for s, (b, _) in enumerate(streams)]
    s_own = each(lambda x, bh: x[0:BS] + table(bown_ref, bh), own, streams)
    gate = [x[BS:BS + nb] + x[BS + nb:BS + 2 * nb]
            + _dot(kmean_parts[b][0], (xq - q.astype(F32)).astype(BF16), 1, 1)
            for x, xq, q, (b, _) in zip(own, q_scaled, qs, streams)]
    s_prev = each(lambda x, bh: x + table(bprev_ref, bh), scores(prev_blk), streams)
    for s, x in enumerate(scores(0)):
        sc_ref[s] = x

    m = each(lambda x: jnp.max(x, axis=0, keepdims=True), s_own)
    for s in range(len(streams)):
        b, rows = acc_at(s)
        acc_ref[b, rows, :] = weighted(i, s, jnp.exp2(s_own[s] - m[s]))

    blk = lax.broadcasted_iota(jnp.int32, (nb, 1), 0).astype(F32)
    i_f = i.astype(F32)
    gate = each(lambda x: jnp.where(blk < i_f, x, -jnp.inf), gate)
    sel = []
    for rnk in range(MOBA_TOPK):
        top = each(lambda x: jnp.max(x, axis=0, keepdims=True), gate)
        idx = each(lambda x, tp: jnp.min(jnp.where(x == tp, blk, float(nb)), axis=0, keepdims=True), gate, top)
        gate = each(lambda x, ix: jnp.where(blk == ix, -jnp.inf, x), gate, idx)
        sel.append(each(lambda ix: jnp.where(rnk < i, ix, -5.0), idx))

    def chosen(s, j):
        j_f = j.astype(F32)
        return (sel[0][s] == j_f) | (sel[1][s] == j_f) | (sel[2][s] == j_f)

    def softmax_step(j, sc, m, valid):
        m_new, alpha = [], []
        for s in range(len(streams)):
            take = chosen(s, j) & valid
            top = jnp.max(sc[s], axis=0, keepdims=True)
            mn = jnp.maximum(m[s], jnp.where(take, top, -jnp.inf))
            p_ref[s] = jnp.exp2(sc[s] - jnp.where(take, mn, jnp.inf)).astype(BF16)
            m_new.append(mn)
            alpha.append(jnp.exp2(m[s] - mn))
        return m_new, alpha

    def accumulate(j, alpha):
        for s in range(len(streams)):
            b, rows = acc_at(s)
            acc_ref[b, rows, :] = alpha[s] * acc_ref[b, rows, :] + weighted(j, s, p_ref[s])

    m, alpha = softmax_step(prev_blk, s_prev, m, i >= 1)

    def far_block(j, st):
        m, alpha = list(st[0]), list(st[1])
        nxt = scores(jnp.minimum(j + 1, nb - 1))
        accumulate(jnp.where(j == 0, prev_blk, j - 1), alpha)
        m, alpha = softmax_step(j, [sc_ref[s] for s in range(len(streams))], m, True)
        for s, x in enumerate(nxt):
            sc_ref[s] = x
        return tuple(m), tuple(alpha)

    n_far = jnp.maximum(i - 1, 0)
    m, alpha = lax.fori_loop(0, n_far, far_block, (tuple(m), tuple(alpha)))
    accumulate(jnp.where(n_far == 0, prev_blk, n_far - 1), alpha)

    outs = []
    for b in range(NB):
        acc = acc_ref[b]
        out_t = jnp.concatenate([acc[h * VR:h * VR + HEAD_DIM] * (1.0 / acc[h * VR + HEAD_DIM:h * VR + HEAD_DIM + 1])
                                 for h in range(2)], axis=0)
        outs.append(out_t.T * _silu(tile(g_ref, b)))
    for b in range(n_batch):
        o_ref[b] = jnp.concatenate(outs[b::n_batch], axis=1).astype(o_ref.dtype)


def _moba(p3, bown, bprev, col0, gcol0, npairs, tiles):
    B, T, _ = p3.shape
    BS = MOBA_BLOCK
    nb = T // BS
    W = tiles * LANES
    assert npairs % tiles == 0 and col0 % tiles == 0 and gcol0 % tiles == 0
    groups = npairs // tiles
    nseq = tiles * B
    return pl.pallas_call(
        _moba_body,
        grid=(groups, nb),
        in_specs=[pl.BlockSpec((B, BS, W), lambda j, i: (0, i, col0 // tiles + j)),
                  pl.BlockSpec((B, BS, W), lambda j, i: (0, i, col0 // tiles + groups + j)),
                  pl.BlockSpec((B, BS, W), lambda j, i: (0, i, col0 // tiles + 2 * groups + j)),
                  pl.BlockSpec((B, BS, W), lambda j, i: (0, i, gcol0 // tiles + j)),
                  pl.BlockSpec((2 * tiles, BS, BS), lambda j, i: (j, 0, 0)),
                  pl.BlockSpec((2 * tiles, BS, BS), lambda j, i: (j, 0, 0))],
        out_specs=pl.BlockSpec((B, BS, W), lambda j, i: (0, i, j)),
        out_shape=jax.ShapeDtypeStruct((B, T, npairs * LANES), BF16),
        scratch_shapes=[pltpu.VMEM((nseq, nb, LANES), F32), pltpu.VMEM((nseq, T, LANES), BF16),
                        pltpu.VMEM((nseq, nb, 2 * MOBA_VROWS, BS), BF16), pltpu.VMEM((nseq, 2 * MOBA_VROWS, BS), F32),
                        pltpu.VMEM((2 * nseq, BS, BS), F32), pltpu.VMEM((2 * nseq, BS, BS), BF16)],
        compiler_params=pltpu.CompilerParams(
            dimension_semantics=("parallel", "arbitrary"), vmem_limit_bytes=VMEM_LIMIT),
        name="moba",
    )(p3, p3, p3, p3, bown, bprev)


def _rel_bucket_of(dist):
    max_exact = REL_BUCKETS // 2
    large = max_exact + (jnp.log(jnp.maximum(dist, 1).astype(F32) / max_exact)
                         / math.log(REL_MAX_DIST / max_exact) * (REL_BUCKETS - max_exact)).astype(jnp.int32)
    large = jnp.minimum(large, REL_BUCKETS - 1)
    return jnp.where(dist < max_exact, dist, large)


def _toeplitz_kq(tab):
    H, L = tab.shape
    BS = L // 2
    flat = jnp.tile(jnp.roll(tab, -1, axis=1), (1, BS))[:, :BS * (L - 1)]
    return flat.reshape(H, BS, L - 1)[:, :, BS - 1:2 * BS - 1]


def _moba_bias_tables(rel_bias):
    BS = MOBA_BLOCK
    per_dist = rel_bias.astype(F32)[_rel_bucket_of(jnp.arange(2 * BS))].T
    per_dist = (per_dist - per_dist[:, 2 * BS - 1:2 * BS]) * LOG2_E
    prev = _toeplitz_kq(per_dist)
    ki = np.arange(BS)[:, None]
    qi = np.arange(BS)[None, :]
    own = jnp.where(ki <= qi, _toeplitz_kq(jnp.roll(per_dist, BS, axis=1)), -jnp.inf)
    return own, prev


def _block_diag(w):
    d, g, n, _ = w.shape
    return jnp.einsum('lgij,gh->lgihj', w, jnp.eye(g, dtype=w.dtype)).reshape(d, g * n, g * n)


def kernel(x, norm_w, w_in, w_out, lru_conv_w, lru_conv_b, lru_gate_a_w, lru_gate_a_b, lru_gate_x_w, lru_gate_x_b, lru_lambda, rwkv_mix, rwkv_w0, rwkv_w_up, rwkv_a0, rwkv_a_up, rwkv_k_k, rwkv_k_a, rwkv_r_k, rwkv_ln_w, rwkv_ln_b, rel_bias, final_norm_w):
    B, T, D = x.shape
    depth = w_in.shape[0]
    lru_w = lru_conv_w.shape[2]
    rwkv_w = rwkv_w0.shape[1]
    moba_w = rel_bias.shape[1] * HEAD_DIM
    rwkv_pairs = rwkv_w // LANES
    moba_pairs = moba_w // LANES
    rwkv_col = (2 * lru_w) // LANES
    rwkv_gate_col = rwkv_col + (3 * rwkv_w + 2 * RWKV_LORA) // LANES
    moba_col = rwkv_gate_col + rwkv_w // LANES
    moba_gate_col = moba_col + (3 * moba_w) // LANES
    n_cols = w_in.shape[2]
    assert (moba_gate_col + moba_w // LANES) * LANES == n_cols
    assert T % MOBA_BLOCK == 0 and 2 * RWKV_LORA == LANES

    M = B * T
    tm_in = 1024
    tn_in = n_cols // 3
    tm_out = 512
    tt_lru = 512
    tt_rwkv = 1024
    moba_tiles = 1

    bown, bprev = _moba_bias_tables(rel_bias)
    lru_vec = jnp.stack([lru_conv_b, lru_gate_a_b, lru_gate_x_b, lru_lambda] + [jnp.zeros_like(lru_lambda)] * 4,
                        axis=1)
    lru_wa = _block_diag(lru_gate_a_w).astype(BF16)
    lru_wx = _block_diag(lru_gate_x_w).astype(BF16)
    mix = rwkv_mix
    rwkv_par = jnp.stack([rwkv_w0, rwkv_a0, rwkv_k_k, rwkv_k_a, rwkv_ln_w, rwkv_ln_b,
                          rwkv_r_k.reshape(depth, rwkv_w), mix[:, 0:rwkv_w], mix[:, rwkv_w:2 * rwkv_w],
                          mix[:, 2 * rwkv_w:3 * rwkv_w]] + [jnp.zeros_like(rwkv_w0)] * 6, axis=1)
    mixwa = mix[:, 3 * rwkv_w:].reshape(depth, 1, LANES)
    zeros_lora = jnp.zeros_like(rwkv_w_up)
    wup = jnp.concatenate([rwkv_w_up, zeros_lora], axis=1).astype(BF16)
    aup = jnp.concatenate([zeros_lora, rwkv_a_up], axis=1).astype(BF16)

    x2 = x.reshape(M, D)
    h, w_in_b, w_out_b = _norm_cast(x2, norm_w[0].reshape(1, D), w_in, w_out, tm_out)
    for l in range(depth):
        final = l == depth - 1
        l_next = None if final else l + 1
        p = _in_proj(h, w_in_b, w_in, l_next, tm_in, tn_in)
        if not final:
            p, w_in_b = p
        p3 = p.reshape(B, T, n_cols)
        ya = _lru(p3, lru_conv_w, lru_vec, lru_wa, lru_wx, l, lru_w, tt_lru)
        yb = _rwkv(p3, rwkv_par, mixwa, wup, aup, l, rwkv_col, rwkv_gate_col, rwkv_pairs, tt_rwkv)
        yc = _moba(p3, bown, bprev, moba_col, moba_gate_col, moba_pairs, moba_tiles)
        nw = (final_norm_w if final else norm_w[l + 1]).reshape(1, D)
        res = _out_proj(x2, ya.reshape(M, lru_w), yb.reshape(M, rwkv_w), yc.reshape(M, moba_w),
                        w_out_b, nw, w_out, l_next, tm_out)
        if final:
            return res.reshape(B, T, D)
        x2, h, w_out_b = res
```

```python
import functools
import math

import numpy as np
import jax
import jax.numpy as jnp
from jax import lax
from jax.experimental import pallas as pl
from jax.experimental.pallas import tpu as pltpu

F32 = jnp.float32
BF16 = jnp.bfloat16

LANES = 128
HEAD_DIM = 64
NORM_EPS = 1e-6
LRU_CONV = 4
LRU_C = 8.0
RWKV_LORA = 64
RWKV_DECAY_SCALE = math.exp(-0.5)
RWKV_GN_EPS = 64e-5
RWKV_CHUNK = 64
MOBA_BLOCK = 256
MOBA_TOPK = 3
MOBA_VROWS = 80
REL_BUCKETS = 32
REL_MAX_DIST = 128
LOG2_E = math.log2(math.e)
VMEM_LIMIT = 60 * 1024 * 1024


def _dot(a, b, ca=1, cb=0):
    return lax.dot_general(a.astype(BF16), b.astype(BF16), (((ca,), (cb,)), ((), ())),
                           preferred_element_type=F32)


def _sigmoid(x):
    return 0.5 * jnp.tanh(0.5 * x) + 0.5


def _silu(x):
    return x * _sigmoid(x)


def _rms_norm(x, w):
    ms = jnp.mean(x * x, axis=-1, keepdims=True)
    return x * lax.rsqrt(ms + NORM_EPS) * w


def _norm_body(x_ref, nw_ref, wi_ref, wo_ref, h_ref, wib_ref, wob_ref):
    h_ref[...] = _rms_norm(x_ref[...], nw_ref[...]).astype(BF16)
    wib_ref[...] = wi_ref[...].astype(BF16)
    wob_ref[...] = wo_ref[...].astype(BF16)


def _slab_rows(D, steps):
    rows = D // steps
    assert rows * steps == D and rows % 16 == 0
    return rows


def _norm_cast(x2, nw, w_in, w_out, tm):
    M, D = x2.shape
    N = w_in.shape[2]
    rows = _slab_rows(D, M // tm)
    return pl.pallas_call(
        _norm_body,
        grid=(M // tm,),
        in_specs=[pl.BlockSpec((tm, D), lambda i: (i, 0)), pl.BlockSpec((1, D), lambda i: (0, 0)),
                  _layer(0, rows, N)(lambda i: (i, 0)), _layer(0, rows, D)(lambda i: (i, 0))],
        out_specs=[pl.BlockSpec((tm, D), lambda i: (i, 0)), pl.BlockSpec((rows, N), lambda i: (i, 0)),
                   pl.BlockSpec((rows, D), lambda i: (i, 0))],
        out_shape=[jax.ShapeDtypeStruct((M, D), BF16), jax.ShapeDtypeStruct((D, N), BF16),
                   jax.ShapeDtypeStruct((D, D), BF16)],
        compiler_params=pltpu.CompilerParams(dimension_semantics=("parallel",), vmem_limit_bytes=VMEM_LIMIT),
        name="norm_cast",
    )(x2, nw, w_in, w_out)


def _layer(l, *block):
    def spec(tail):
        return pl.BlockSpec((None,) + block, lambda *g: (l,) + tail(*g))
    return spec


def _inproj_body(h_ref, w_ref, *rest):
    if len(rest) == 3:
        wn_ref, o_ref, wnb_ref = rest
        wnb_ref[...] = wn_ref[...].astype(BF16)
    else:
        (o_ref,) = rest
    o_ref[...] = jnp.dot(h_ref[...], w_ref[...], preferred_element_type=F32)


def _in_proj(h, w, w_all, l_next, tm, tn):
    M, D = h.shape
    N = w.shape[1]
    in_specs = [pl.BlockSpec((tm, D), lambda j, i: (i, 0)), pl.BlockSpec((D, tn), lambda j, i: (0, j))]
    out_specs = pl.BlockSpec((tm, tn), lambda j, i: (i, j))
    out_shape = jax.ShapeDtypeStruct((M, N), F32)
    args = (h, w)
    if l_next is not None:
        rows = _slab_rows(D, M // tm)
        in_specs.append(_layer(l_next, rows, tn)(lambda j, i: (i, j)))
        out_specs = [out_specs, pl.BlockSpec((rows, tn), lambda j, i: (i, j))]
        out_shape = [out_shape, jax.ShapeDtypeStruct((D, N), BF16)]
        args += (w_all,)
    return pl.pallas_call(
        _inproj_body,
        grid=(N // tn, M // tm),
        in_specs=in_specs,
        out_specs=out_specs,
        out_shape=out_shape,
        compiler_params=pltpu.CompilerParams(
            dimension_semantics=("parallel", "parallel"), vmem_limit_bytes=VMEM_LIMIT),
        name="in_proj" if l_next is not None else "in_proj_last",
    )(*args)


def _outproj_body(x_ref, ya_ref, yb_ref, yc_ref, w_ref, nw_ref, *rest, final):
    wa, wb = ya_ref.shape[1], yb_ref.shape[1]
    acc = x_ref[...]
    acc = acc + _dot(ya_ref[...], w_ref[0:wa, :])
    acc = acc + _dot(yb_ref[...], w_ref[wa:wa + wb, :])
    acc = acc + _dot(yc_ref[...], w_ref[wa + wb:, :])
    normed = _rms_norm(acc, nw_ref[...])
    if final:
        rest[0][...] = normed
    else:
        wn_ref, x_out_ref, h_ref, wnb_ref = rest
        x_out_ref[...] = acc
        h_ref[...] = normed.astype(BF16)
        wnb_ref[...] = wn_ref[...].astype(BF16)


def _out_proj(x2, ya, yb, yc, w, nw, w_all, l_next, tm):
    M, D = x2.shape
    final = l_next is None
    row = lambda width: pl.BlockSpec((tm, width), lambda i: (i, 0))
    full = lambda a: pl.BlockSpec(a.shape, lambda i: (0, 0))
    in_specs = [row(D), row(ya.shape[1]), row(yb.shape[1]), row(yc.shape[1]), full(w), full(nw)]
    args = (x2, ya, yb, yc, w, nw)
    if final:
        out_specs, out_shape = row(D), jax.ShapeDtypeStruct((M, D), F32)
    else:
        rows = _slab_rows(D, M // tm)
        in_specs.append(_layer(l_next, rows, D)(lambda i: (i, 0)))
        args += (w_all,)
        out_specs = [row(D), row(D), pl.BlockSpec((rows, D), lambda i: (i, 0))]
        out_shape = [jax.ShapeDtypeStruct((M, D), F32), jax.ShapeDtypeStruct((M, D), BF16),
                     jax.ShapeDtypeStruct((D, D), BF16)]
    return pl.pallas_call(
        functools.partial(_outproj_body, final=final),
        grid=(M // tm,),
        in_specs=in_specs,
        out_specs=out_specs,
        out_shape=out_shape,
        compiler_params=pltpu.CompilerParams(
            dimension_semantics=("parallel",), vmem_limit_bytes=VMEM_LIMIT),
        name="out_proj_final" if final else "out_proj",
    )(*args)


def _lru_body(xa_ref, ga_ref, cw_ref, vec_ref, wa_ref, wx_ref, o_ref, xbuf, hcar):
    tt = xa_ref.shape[1]
    W = xa_ref.shape[2]

    @pl.when(pl.program_id(1) == 0)
    def _():
        xbuf[0:8, :] = jnp.zeros((8, W), F32)
        hcar[...] = jnp.zeros((8, W), F32)

    xa = xa_ref[0]
    xbuf[8:8 + tt, :] = xa
    xc = vec_ref[0:1, :] + cw_ref[LRU_CONV - 1:LRU_CONV, :] * xa
    for s in range(1, LRU_CONV):
        xc = xc + cw_ref[LRU_CONV - 1 - s:LRU_CONV - s, :] * xbuf[8 - s:8 - s + tt, :]
    xbuf[0:8, :] = xa[tt - 8:tt, :]

    r = _sigmoid(_dot(xc, wa_ref[...]) + vec_ref[1:2, :])
    gi = _sigmoid(_dot(xc, wx_ref[...]) + vec_ref[2:3, :])
    lam = vec_ref[3:4, :]
    softplus_neg_lam = jnp.maximum(-lam, 0.0) + jnp.log1p(jnp.exp(-jnp.abs(lam)))
    log_a = (-LRU_C) * r * softplus_neg_lam
    a = jnp.exp(log_a)
    one_minus_a2 = -jnp.tanh(log_a) * (a * a + 1.0)
    b = jnp.where(one_minus_a2 > 0.0, one_minus_a2 * lax.rsqrt(one_minus_a2), 0.0) * (gi * xc)

    row = lax.broadcasted_iota(jnp.int32, (tt, 1), 0)
    d = 1
    while d < tt:
        keep = row >= d
        a_sh = pltpu.roll(a, d, 0)
        b_sh = pltpu.roll(b, d, 0)
        b = b + jnp.where(keep, a * b_sh, 0.0)
        a = jnp.where(keep, a * a_sh, a)
        d *= 2
    h = b + a * hcar[0:1, :]
    hcar[...] = jnp.broadcast_to(h[tt - 1:tt, :], (8, W))
    o_ref[0] = (h * _silu(ga_ref[0])).astype(o_ref.dtype)


def _lru(p3, cw, vec, wa, wx, l, width, tt):
    B, T, _ = p3.shape
    full = lambda a: _layer(l, *a.shape[1:])(lambda b, t: (0, 0))
    return pl.pallas_call(
        _lru_body,
        grid=(B, T // tt),
        in_specs=[pl.BlockSpec((1, tt, width), lambda b, t: (b, t, 0)),
                  pl.BlockSpec((1, tt, width), lambda b, t: (b, t, 1)),
                  full(cw), full(vec), full(wa), full(wx)],
        out_specs=pl.BlockSpec((1, tt, width), lambda b, t: (b, t, 0)),
        out_shape=jax.ShapeDtypeStruct((B, T, width), BF16),
        scratch_shapes=[pltpu.VMEM((tt + 8, width), F32), pltpu.VMEM((8, width), F32)],
        compiler_params=pltpu.CompilerParams(
            dimension_semantics=("parallel", "arbitrary"), vmem_limit_bytes=VMEM_LIMIT),
        name="rg_lru",
    )(p3, p3, cw, vec, wa, wx)


def _rwkv_body(r_ref, k_ref, v_ref, wa_ref, g_ref, par_ref, mixwa_ref, wup_ref, aup_ref, o_ref,
               xs_ref, s_ref):
    C = RWKV_CHUNK
    NB, tt = r_ref.shape[0], r_ref.shape[1]
    nc = tt // C

    @pl.when(pl.program_id(1) == 0)
    def _():
        xs_ref[:, 0:8, :] = jnp.zeros((xs_ref.shape[0], 8, LANES), F32)
        s_ref[...] = jnp.zeros(s_ref.shape, F32)

    lane = lax.broadcasted_iota(jnp.int32, (1, LANES), 1)
    head0 = lane < HEAD_DIM
    rc = lax.broadcasted_iota(jnp.int32, (tt, 1), 0) & (C - 1)

    par = par_ref[...]
    w0, a0, k_k, k_a = par[0:1], par[1:2], par[2:3], par[3:4]
    ln_w, ln_b, r_k = par[4:5], par[5:6], par[6:7]
    mix_r, mix_k, mix_v = par[7:8], par[8:9], par[9:10]

    def shift_lerp(ref, b, mix, slot):
        s = ref[b]
        xs_ref[4 * b + slot, 8:8 + tt, :] = s
        prev = xs_ref[4 * b + slot, 7:7 + tt, :]
        xs_ref[4 * b + slot, 0:8, :] = s[tt - 8:tt, :]
        return s + mix * (prev - s)

    def headsum(x):
        s0 = jnp.sum(jnp.where(head0, x, 0.0), axis=-1, keepdims=True)
        s1 = jnp.sum(jnp.where(head0, 0.0, x), axis=-1, keepdims=True)
        return jnp.where(head0, s0, s1)

    def chunks(x):
        return [x[ch * C:(ch + 1) * C] for ch in range(nc)]

    def front(b):
        r = shift_lerp(r_ref, b, mix_r, 0)
        k = shift_lerp(k_ref, b, mix_k, 1)
        v = shift_lerp(v_ref, b, mix_v, 2)
        wa = shift_lerp(wa_ref, b, mixwa_ref[...], 3)
        log_w = (-RWKV_DECAY_SCALE) * _sigmoid(w0 + _dot(jnp.tanh(wa), wup_ref[...]))
        a = _sigmoid(a0 + _dot(wa, aup_ref[...]))
        kk = k * k_k
        kk = kk * lax.rsqrt(jnp.maximum(headsum(kk * kk), 1e-24))
        k = k * (1.0 + (a - 1.0) * k_a)
        be = kk * a
        c = log_w
        d = 1
        while d < C:
            c = c + jnp.where(rc >= d, pltpu.roll(c, d, 0), 0.0)
            d *= 2
        c_last = jnp.broadcast_to(c.reshape(nc, C, LANES)[:, C - 1:C, :], (nc, C, LANES)).reshape(tt, LANES)
        e_inc = jnp.exp(c)
        e_inv = jnp.exp(-c)
        e_end = jnp.exp(c_last - c)
        parts = dict(a=-kk * jnp.exp(c - log_w), r=r * e_inc, kt=k * e_inv, bt=be * e_inv,
                     kd=k * e_end, bd=be * e_end, v=v, gamma_end=jnp.exp(c_last))
        return {name: chunks(val) for name, val in parts.items()}, headsum(r * k * r_k) * v

    fronts = [front(b) for b in range(NB)]
    gather = lambda name: [x for f, _ in fronts for x in f[name]]

    ti = lax.broadcasted_iota(jnp.int32, (C, LANES), 0)
    si = lax.broadcasted_iota(jnp.int32, (C, LANES), 1) & (C - 1)
    strict = ti > si
    incl = ti >= si
    same16 = (ti >> 4) == (si >> 4)
    same32 = (ti >> 5) == (si >> 5)
    eye = (ti == si).astype(F32)
    same_head = ((lax.broadcasted_iota(jnp.int32, (LANES, LANES), 0) >> 6)
                 == (lax.broadcasted_iota(jnp.int32, (LANES, LANES), 1) >> 6))
    head0_b = jnp.broadcast_to(head0, (C, LANES))

    def embed(x):
        xb = x.astype(BF16)
        zero = jnp.zeros_like(xb)
        return jnp.concatenate([jnp.where(head0_b, xb, zero), jnp.where(head0_b, zero, xb)], axis=0)

    def each(f, *lists):
        return [f(*xs) for xs in zip(*lists)]

    lo = lambda x: x[:, 0:LANES]
    hi = lambda x: x[:, LANES:2 * LANES]
    side = lambda x, y: jnp.concatenate([x, y], axis=1)

    a_c = gather('a')
    r_c = gather('r')
    k_e = each(embed, gather('kt'))
    b_e = each(embed, gather('bt'))
    kd_c = gather('kd')
    bd_c = gather('bd')
    v_c = gather('v')
    v_e = each(embed, v_c)
    gamma_end = gather('gamma_end')

    p = each(lambda aa, rr, kt, bt: _dot(jnp.concatenate([aa, rr], axis=0),
                                         jnp.concatenate([kt, bt], axis=0), 1, 1), a_c, r_c, k_e, b_e)
    a_ak = each(lambda x: jnp.where(strict, x[0:C, 0:LANES], 0.0), p)
    n = each(lambda x: jnp.where(strict, x[0:C, LANES:2 * LANES], 0.0), p)
    a_rk = each(lambda x: jnp.where(incl, x[C:2 * C, 0:LANES], 0.0), p)
    a_rb = each(lambda x: jnp.where(incl, x[C:2 * C, LANES:2 * LANES], 0.0), p)
    q = each(lambda x: jnp.where(same16, x, 0.0), n)
    t = each(lambda x: eye + x, q)
    q = each(lambda x: _dot(x, embed(x)), q)
    for _ in range(2):
        qt = each(lambda x, y: _dot(x, side(embed(y), embed(x))), q, t)
        t = each(lambda x, y: x + lo(y), t, qt)
        q = each(hi, qt)
    t = each(lambda x, y: x + _dot(y, embed(x)), t, q)
    for half, off_diag in ((16, same32 & jnp.logical_not(same16)), (32, jnp.logical_not(same32))):
        lower = lambda x: jnp.concatenate([x[r:r + half] for r in range(half, C, 2 * half)], axis=0)

        def add_lower(x, upd):
            parts = []
            for k, r in enumerate(range(0, C, 2 * half)):
                parts += [x[r:r + half], x[r + half:r + 2 * half] + upd[k * half:(k + 1) * half]]
            return jnp.concatenate(parts, axis=0)

        tn = each(lambda x, y: _dot(lower(x), embed(jnp.where(off_diag, y, 0.0))), t, n)
        t = each(lambda x, y: add_lower(x, _dot(y, embed(x))), t, tn)
    av = each(lambda x, y, vv: _dot(jnp.concatenate([x, y], axis=0), vv), a_ak, a_rk, v_e)
    taw = each(lambda x, aa, w: _dot(x, side(embed(aa), embed(w[0:C]))), t, a_c, av)
    ry = each(lambda x, y: _dot(x, side(embed(lo(y)), embed(hi(y)))), a_rb, taw)
    rq = each(lambda rr, x: rr + lo(x), r_c, ry)
    y0 = each(lambda w, x: w[C:2 * C] + hi(x), av, ry)
    g = each(lambda x, bd: jnp.where(same_head, _dot(lo(x), bd, 0, 0), 0.0), taw, bd_c)
    s0 = each(lambda vv, x, kd, bd: jnp.where(same_head, _dot(jnp.concatenate([vv, hi(x)], axis=0),
                                                              jnp.concatenate([kd, bd], axis=0), 0, 0), 0.0),
              v_c, taw, kd_c, bd_c)

    states = [s_ref[b] for b in range(NB)]
    ys = [[] for _ in range(NB)]
    for ch in range(nc):
        for b in range(NB):
            i, s = b * nc + ch, states[b]
            ys[b].append(_dot(rq[i], s, 1, 1) + y0[i])
            states[b] = s * gamma_end[i][C - 1:C, :] + _dot(s, g[i]) + s0[i]
    for b in range(NB):
        s_ref[b] = states[b]
        y = jnp.concatenate(ys[b], axis=0)
        mu = headsum(y) * (1.0 / HEAD_DIM)
        yc = y - mu
        var = headsum(yc * yc) * (1.0 / HEAD_DIM)
        yn = yc * lax.rsqrt(var + RWKV_GN_EPS) * ln_w + ln_b
        o_ref[b] = ((yn + fronts[b][1]) * _silu(g_ref[b])).astype(o_ref.dtype)


def _rwkv(p3, par, mixwa, wup, aup, l, col0, gcol0, npairs, tt):
    B, T, _ = p3.shape
    col = lambda off: pl.BlockSpec((B, tt, LANES), lambda j, t: (0, t, off + j))
    fixed = lambda off: pl.BlockSpec((B, tt, LANES), lambda j, t: (0, t, off))
    return pl.pallas_call(
        _rwkv_body,
        grid=(npairs, T // tt),
        in_specs=[col(col0), col(col0 + npairs), col(col0 + 2 * npairs), fixed(col0 + 3 * npairs),
                  col(gcol0),
                  _layer(l, 16, LANES)(lambda j, t: (0, j)),
                  _layer(l, 1, LANES)(lambda j, t: (0, 0)),
                  _layer(l, LANES, LANES)(lambda j, t: (0, j)),
                  _layer(l, LANES, LANES)(lambda j, t: (0, j))],
        out_specs=pl.BlockSpec((B, tt, LANES), lambda j, t: (0, t, j)),
        out_shape=jax.ShapeDtypeStruct((B, T, npairs * LANES), BF16),
        scratch_shapes=[pltpu.VMEM((4 * B, tt + 8, LANES), F32), pltpu.VMEM((B, LANES, LANES), F32)],
        compiler_params=pltpu.CompilerParams(
            dimension_semantics=("parallel", "arbitrary"), vmem_limit_bytes=VMEM_LIMIT),
        name="rwkv7",
    )(p3, p3, p3, p3, p3, par, mixwa, wup, aup)


def _moba_body(q_ref, k_ref, v_ref, g_ref, bown_ref, bprev_ref, o_ref, kmean_ref, kb_ref, vt_ref, acc_ref,
               sc_ref, p_ref):
    BS = MOBA_BLOCK
    n_batch, T = k_ref.shape[0], kb_ref.shape[1]
    NB = n_batch * (k_ref.shape[2] // LANES)
    nb = T // BS
    i = pl.program_id(1)
    lane = lax.broadcasted_iota(jnp.int32, (1, LANES), 1)
    head0 = lane < HEAD_DIM
    streams = [(b, h) for b in range(NB) for h in range(2)]

    def tile(ref, u):
        return ref[u % n_batch][:, (u // n_batch) * LANES:(u // n_batch + 1) * LANES]

    def table(ref, bh):
        return ref[2 * (bh[0] // n_batch) + bh[1]]

    def each(f, *lists):
        return [f(*xs) for xs in zip(*lists)]

    VR = MOBA_VROWS

    @pl.when(i == 0)
    def _():
        kmean_ref[...] = jnp.zeros(kmean_ref.shape, F32)

    own_rows = pl.ds(pl.multiple_of(i * BS, BS), BS)
    ones = jnp.ones((VR - HEAD_DIM, BS), BF16)
    is_own = lax.broadcasted_iota(jnp.int32, (nb, 1), 0) == i
    for b in range(NB):
        k_blk = tile(k_ref, b)
        kmean_ref[b] = jnp.where(is_own, jnp.mean(k_blk, axis=0, keepdims=True), kmean_ref[b])
        kb_ref[b, own_rows, :] = k_blk.astype(BF16)
        vt = tile(v_ref, b).T.astype(BF16)
        vt_ref[b, i] = jnp.concatenate([vt[0:HEAD_DIM], ones, vt[HEAD_DIM:2 * HEAD_DIM], ones], axis=0)

    qh = []
    for b in range(NB):
        q = tile(q_ref, b)
        qh += [jnp.where(head0, q, 0.0), jnp.where(head0, 0.0, q)]
    q_scaled = [x * (HEAD_DIM ** -0.5 * LOG2_E) for x in qh]
    qs = [x.astype(BF16) for x in q_scaled]

    def scores(j):
        rows = pl.ds(pl.multiple_of(j * BS, BS), BS)
        return [_dot(kb_ref[b, rows, :], qs[s], 1, 1) for s, (b, _) in enumerate(streams)]

    def weighted(j, s, p):
        b, h = streams[s]
        return _dot(vt_ref[b, j, h * VR:(h + 1) * VR, :], p)

    def acc_at(s):
        b, h = streams[s]
        return b, slice(h * VR, (h + 1) * VR)

    def split(x):
        hi = x.astype(BF16)
        return hi, (x - hi.astype(F32)).astype(BF16)

    prev_blk = jnp.maximum(i - 1, 0)
    kmean_parts = [split(kmean_ref[b]) for b in range(NB)]
    own = [_dot(jnp.concatenate([kb_ref[b, own_rows, :], kmean_parts[b][0], kmean_parts[b][1]], axis=0), qs[s], 1, 1)
           for s, (b, _) in enumerate(streams)]
    s_own = each(lambda x, bh: x[0:BS] + table(bown_ref, bh), own, streams)
    gate = [x[BS:BS + nb] + x[BS + nb:BS + 2 * nb]
            + _dot(kmean_parts[b][0], (xq - q.astype(F32)).astype(BF16), 1, 1)
            for x, xq, q, (b, _) in zip(own, q_scaled, qs, streams)]
    for s, x in enumerate(scores(0)):
        sc_ref[s] = x

    m = each(lambda x: jnp.max(x, axis=0, keepdims=True), s_own)
    for s in range(len(streams)):
        b, rows = acc_at(s)
        acc_ref[b, rows, :] = weighted(i, s, jnp.exp2(s_own[s] - m[s]))

    blk = lax.broadcasted_iota(jnp.int32, (nb, 1), 0).astype(F32)
    i_f = i.astype(F32)
    gate = each(lambda x: jnp.where(blk < i_f, x, -jnp.inf), gate)
    sel = []
    for rnk in range(MOBA_TOPK):
        top = each(lambda x: jnp.max(x, axis=0, keepdims=True), gate)
        idx = each(lambda x, tp: jnp.min(jnp.where(x == tp, blk, float(nb)), axis=0, keepdims=True), gate, top)
        gate = each(lambda x, ix: jnp.where(blk == ix, -jnp.inf, x), gate, idx)
        sel.append(each(lambda ix: jnp.where(rnk < i, ix, -5.0), idx))

    def chosen(s, j):
        j_f = j.astype(F32)
        return (sel[0][s] == j_f) | (sel[1][s] == j_f) | (sel[2][s] == j_f)

    def softmax_step(j, sc, m, valid):
        m_new, alpha = [], []
        for s in range(len(streams)):
            take = chosen(s, j) & valid
            top = jnp.max(sc[s], axis=0, keepdims=True)
            mn = jnp.maximum(m[s], jnp.where(take, top, -jnp.inf))
            p_ref[s] = jnp.exp2(sc[s] - jnp.where(take, mn, jnp.inf)).astype(BF16)
            m_new.append(mn)
            alpha.append(jnp.exp2(m[s] - mn))
        return m_new, alpha

    def accumulate(j, alpha):
        for s in range(len(streams)):
            b, rows = acc_at(s)
            acc_ref[b, rows, :] = alpha[s] * acc_ref[b, rows, :] + weighted(j, s, p_ref[s])

    for s in range(len(streams)):
        p_ref[s] = jnp.zeros((BS, BS), BF16)
    alpha = [jnp.ones((1, BS), F32)] * len(streams)

    def far_block(j, st):
        m, alpha = list(st[0]), list(st[1])
        nxt = scores(jnp.minimum(j + 1, nb - 1))
        accumulate(jnp.maximum(j - 1, 0), alpha)
        m, alpha = softmax_step(j, [sc_ref[s] for s in range(len(streams))], m, True)
        for s, x in enumerate(nxt):
            sc_ref[s] = x
        return tuple(m), tuple(alpha)

    n_far = jnp.maximum(i - 1, 0)
    m, alpha = lax.fori_loop(0, n_far, far_block, (tuple(m), tuple(alpha)))

    for s, bh in enumerate(streams):
        b, rows = acc_at(s)
        pending = weighted(jnp.maximum(n_far - 1, 0), s, p_ref[s])
        sc = sc_ref[s] + table(bprev_ref, bh)
        take = chosen(s, prev_blk) & (i >= 1)
        mn = jnp.maximum(m[s], jnp.where(take, jnp.max(sc, axis=0, keepdims=True), -jnp.inf))
        p = jnp.exp2(sc - jnp.where(take, mn, jnp.inf))
        acc_ref[b, rows, :] = (jnp.exp2(m[s] - mn) * (alpha[s] * acc_ref[b, rows, :] + pending)
                               + weighted(prev_blk, s, p))

    outs = []
    for b in range(NB):
        acc = acc_ref[b]
        out_t = jnp.concatenate([acc[h * VR:h * VR + HEAD_DIM] * (1.0 / acc[h * VR + HEAD_DIM:h * VR + HEAD_DIM + 1])
                                 for h in range(2)], axis=0)
        outs.append(out_t.T * _silu(tile(g_ref, b)))
    for b in range(n_batch):
        o_ref[b] = jnp.concatenate(outs[b::n_batch], axis=1).astype(o_ref.dtype)


def _moba(p3, bown, bprev, col0, gcol0, npairs, tiles):
    B, T, _ = p3.shape
    BS = MOBA_BLOCK
    nb = T // BS
    W = tiles * LANES
    assert npairs % tiles == 0 and col0 % tiles == 0 and gcol0 % tiles == 0
    groups = npairs // tiles
    nseq = tiles * B
    return pl.pallas_call(
        _moba_body,
        grid=(groups, nb),
        in_specs=[pl.BlockSpec((B, BS, W), lambda j, i: (0, i, col0 // tiles + j)),
                  pl.BlockSpec((B, BS, W), lambda j, i: (0, i, col0 // tiles + groups + j)),
                  pl.BlockSpec((B, BS, W), lambda j, i: (0, i, col0 // tiles + 2 * groups + j)),
                  pl.BlockSpec((B, BS, W), lambda j, i: (0, i, gcol0 // tiles + j)),
                  pl.BlockSpec((2 * tiles, BS, BS), lambda j, i: (j, 0, 0)),
                  pl.BlockSpec((2 * tiles, BS, BS), lambda j, i: (j, 0, 0))],
        out_specs=pl.BlockSpec((B, BS, W), lambda j, i: (0, i, j)),
        out_shape=jax.ShapeDtypeStruct((B, T, npairs * LANES), BF16),
        scratch_shapes=[pltpu.VMEM((nseq, nb, LANES), F32), pltpu.VMEM((nseq, T, LANES), BF16),
                        pltpu.VMEM((nseq, nb, 2 * MOBA_VROWS, BS), BF16), pltpu.VMEM((nseq, 2 * MOBA_VROWS, BS), F32),
                        pltpu.VMEM((2 * nseq, BS, BS), F32), pltpu.VMEM((2 * nseq, BS, BS), BF16)],
        compiler_params=pltpu.CompilerParams(
            dimension_semantics=("parallel", "arbitrary"), vmem_limit_bytes=VMEM_LIMIT),
        name="moba",
    )(p3, p3, p3, p3, bown, bprev)


def _rel_bucket_of(dist):
    max_exact = REL_BUCKETS // 2
    large = max_exact + (jnp.log(jnp.maximum(dist, 1).astype(F32) / max_exact)
                         / math.log(REL_MAX_DIST / max_exact) * (REL_BUCKETS - max_exact)).astype(jnp.int32)
    large = jnp.minimum(large, REL_BUCKETS - 1)
    return jnp.where(dist < max_exact, dist, large)


def _toeplitz_kq(tab):
    H, L = tab.shape
    BS = L // 2
    flat = jnp.tile(jnp.roll(tab, -1, axis=1), (1, BS))[:, :BS * (L - 1)]
    return flat.reshape(H, BS, L - 1)[:, :, BS - 1:2 * BS - 1]


def _moba_bias_tables(rel_bias):
    BS = MOBA_BLOCK
    per_dist = rel_bias.astype(F32)[_rel_bucket_of(jnp.arange(2 * BS))].T
    per_dist = (per_dist - per_dist[:, 2 * BS - 1:2 * BS]) * LOG2_E
    prev = _toeplitz_kq(per_dist)
    ki = np.arange(BS)[:, None]
    qi = np.arange(BS)[None, :]
    own = jnp.where(ki <= qi, _toeplitz_kq(jnp.roll(per_dist, BS, axis=1)), -jnp.inf)
    return own, prev


def _block_diag(w):
    d, g, n, _ = w.shape
    return jnp.einsum('lgij,gh->lgihj', w, jnp.eye(g, dtype=w.dtype)).reshape(d, g * n, g * n)


def kernel(x, norm_w, w_in, w_out, lru_conv_w, lru_conv_b, lru_gate_a_w, lru_gate_a_b, lru_gate_x_w, lru_gate_x_b, lru_lambda, rwkv_mix, rwkv_w0, rwkv_w_up, rwkv_a0, rwkv_a_up, rwkv_k_k, rwkv_k_a, rwkv_r_k, rwkv_ln_w, rwkv_ln_b, rel_bias, final_norm_w):
    B, T, D = x.shape
    depth = w_in.shape[0]
    lru_w = lru_conv_w.shape[2]
    rwkv_w = rwkv_w0.shape[1]
    moba_w = rel_bias.shape[1] * HEAD_DIM
    rwkv_pairs = rwkv_w // LANES
    moba_pairs = moba_w // LANES
    rwkv_col = (2 * lru_w) // LANES
    rwkv_gate_col = rwkv_col + (3 * rwkv_w + 2 * RWKV_LORA) // LANES
    moba_col = rwkv_gate_col + rwkv_w // LANES
    moba_gate_col = moba_col + (3 * moba_w) // LANES
    n_cols = w_in.shape[2]
    assert (moba_gate_col + moba_w // LANES) * LANES == n_cols
    assert T % MOBA_BLOCK == 0 and 2 * RWKV_LORA == LANES

    M = B * T
    tm_in = 1024
    tn_in = n_cols // 3
    tm_out = 512
    tt_lru = 512
    tt_rwkv = 1024
    moba_tiles = 1

    bown, bprev = _moba_bias_tables(rel_bias)
    lru_vec = jnp.stack([lru_conv_b, lru_gate_a_b, lru_gate_x_b, lru_lambda] + [jnp.zeros_like(lru_lambda)] * 4,
                        axis=1)
    lru_wa = _block_diag(lru_gate_a_w).astype(BF16)
    lru_wx = _block_diag(lru_gate_x_w).astype(BF16)
    mix = rwkv_mix
    rwkv_par = jnp.stack([rwkv_w0, rwkv_a0, rwkv_k_k, rwkv_k_a, rwkv_ln_w, rwkv_ln_b,
                          rwkv_r_k.reshape(depth, rwkv_w), mix[:, 0:rwkv_w], mix[:, rwkv_w:2 * rwkv_w],
                          mix[:, 2 * rwkv_w:3 * rwkv_w]] + [jnp.zeros_like(rwkv_w0)] * 6, axis=1)
    mixwa = mix[:, 3 * rwkv_w:].reshape(depth, 1, LANES)
    zeros_lora = jnp.zeros_like(rwkv_w_up)
    wup = jnp.concatenate([rwkv_w_up, zeros_lora], axis=1).astype(BF16)
    aup = jnp.concatenate([zeros_lora, rwkv_a_up], axis=1).astype(BF16)

    x2 = x.reshape(M, D)
    h, w_in_b, w_out_b = _norm_cast(x2, norm_w[0].reshape(1, D), w_in, w_out, tm_out)
    for l in range(depth):
        final = l == depth - 1
        l_next = None if final else l + 1
        p = _in_proj(h, w_in_b, w_in, l_next, tm_in, tn_in)
        if not final:
            p, w_in_b = p
        p3 = p.reshape(B, T, n_cols)
        ya = _lru(p3, lru_conv_w, lru_vec, lru_wa, lru_wx, l, lru_w, tt_lru)
        yb = _rwkv(p3, rwkv_par, mixwa, wup, aup, l, rwkv_col, rwkv_gate_col, rwkv_pairs, tt_rwkv)
        yc = _moba(p3, bown, bprev, moba_col, moba_gate_col, moba_pairs, moba_tiles)
        nw = (final_norm_w if final else norm_w[l + 1]).reshape(1, D)
        res = _out_proj(x2, ya.reshape(M, lru_w), yb.reshape(M, rwkv_w), yc.reshape(M, moba_w),
                        w_out_b, nw, w_out, l_next, tm_out)
        if final:
            return res.reshape(B, T, D)
        x2, h, w_out_b = res
```
